```python
import math
import jax, jax.numpy as jnp
from jax import lax
import numpy as np

D_MODEL = 1024
BATCH = 4
SEQ = 8192
DEPTH = 2

PLE_DIM = 256
MIX_WIDTH = D_MODEL
RET_HEAD_DIM = 128
RET_HEADS = (MIX_WIDTH // 2) // RET_HEAD_DIM
RET_WIDTH = RET_HEADS * RET_HEAD_DIM
MOBA_HEAD_DIM = 128
MOBA_HEADS = (MIX_WIDTH - RET_WIDTH) // MOBA_HEAD_DIM
MOBA_WIDTH = MOBA_HEADS * MOBA_HEAD_DIM
SPLIT_SIZES = [RET_WIDTH] * 4 + [MOBA_WIDTH] * 4
IN_WIDTH = sum(SPLIT_SIZES)
RET_CHUNK = 128
MOBA_BLOCK = 256
MOBA_TOPK = 3
MOBA_Q_CHUNK = 64
N_BUCKETS = 32
MAX_DISTANCE = 2048
ROPE_THETA = 10000.0
EPS = 1e-6
NEG_INF = -1e30

kernel_name = "hymba_retention_moba_hybrid"


def rms_norm(x, g=None):
    xf = x.astype(jnp.float32)
    y = xf * lax.rsqrt(jnp.mean(xf * xf, axis=-1, keepdims=True) + EPS)
    if g is not None:
        y = y * g.astype(jnp.float32)
    return y.astype(x.dtype)


def rotary(x, pos):
    half = x.shape[-1] // 2
    inv = ROPE_THETA ** (-jnp.arange(half, dtype=jnp.float32) / half)
    ang = pos.astype(jnp.float32)[:, None] * inv[None, :]
    cos = jnp.cos(ang)[None, :, None, :]
    sin = jnp.sin(ang)[None, :, None, :]
    x1, x2 = x[..., :half], x[..., half:]
    return jnp.concatenate([x1 * cos - x2 * sin, x1 * sin + x2 * cos], axis=-1)


def t5_bucket(n):
    max_exact = N_BUCKETS // 2
    nf = jnp.maximum(n, 1).astype(jnp.float32)
    large = max_exact + (jnp.log(nf / max_exact) / math.log(MAX_DISTANCE / max_exact)
                         * (N_BUCKETS - max_exact)).astype(jnp.int32)
    large = jnp.minimum(large, N_BUCKETS - 1)
    return jnp.where(n < max_exact, n, large)


def retention(q, k, v, gain):
    B, S, H, d = q.shape
    C = RET_CHUNK
    nc = S // C
    log_g = jnp.log1p(-jnp.power(2.0, -5.0 - jnp.arange(H, dtype=jnp.float32)))

    def chunks(t):
        return t.reshape(B, nc, C, H, d).transpose(0, 3, 1, 2, 4)

    qc, kc, vc = chunks(q), chunks(k * (d ** -0.5)), chunks(v)
    j = jnp.arange(C, dtype=jnp.float32)
    rel = j[:, None] - j[None, :]
    decay = jnp.where(rel >= 0, jnp.exp(log_g[:, None, None] * jnp.maximum(rel, 0.0)[None]), 0.0)
    s = jnp.einsum('bhcnd,bhcmd->bhcnm', qc, kc) * decay[None, :, None]
    inner = jnp.einsum('bhcnm,bhcme->bhcne', s, vc)
    k_dec = kc * jnp.exp(log_g[:, None] * (C - 1 - j)[None])[None, :, None, :, None]
    kv = jnp.einsum('bhcmd,bhcme->cbhde', k_dec, vc)
    chunk_decay = jnp.exp(log_g * C)[None, :, None, None]

    def step(state, kv_c):
        return state * chunk_decay + kv_c, state

    _, prev = lax.scan(step, jnp.zeros((B, H, d, d), jnp.float32), kv)
    q_dec = qc * jnp.exp(log_g[:, None] * (j + 1.0)[None])[None, :, None, :, None]
    cross = jnp.einsum('bhcnd,cbhde->bhcne', q_dec, prev)
    o = (inner + cross).transpose(0, 2, 3, 1, 4).reshape(B, S, H, d)
    mu = jnp.mean(o, axis=-1, keepdims=True)
    var = jnp.mean(jnp.square(o - mu), axis=-1, keepdims=True)
    o = (o - mu) * lax.rsqrt(var + EPS)
    return o.reshape(B, S, H * d) * gain.astype(jnp.float32)


def moba(q, k, v, rel_bias):
    B, S, H, d = q.shape
    BLK, QC = MOBA_BLOCK, MOBA_Q_CHUNK
    nb = -(-S // BLK)
    pad = nb * BLK - S
    q = q.transpose(0, 2, 1, 3) * (d ** -0.5)
    kp = jnp.pad(k.transpose(0, 2, 1, 3), ((0, 0), (0, 0), (0, pad), (0, 0)))
    vp = jnp.pad(v.transpose(0, 2, 1, 3), ((0, 0), (0, 0), (0, pad), (0, 0)))
    kb = kp.reshape(B, H, nb, BLK, d)
    vb = vp.reshape(B, H, nb, BLK, d)
    kmean = jnp.mean(kb, axis=3)

    qpos = jnp.arange(S)
    qblock = qpos // BLK
    gate = jnp.einsum('bhsd,bhnd->bhsn', q, kmean)
    past = jnp.arange(nb)[None, :] < qblock[:, None]
    gate = jnp.where(past[None, None], gate, NEG_INF)
    n_sel = min(MOBA_TOPK, nb)
    _, top_idx = lax.top_k(gate, n_sel)

    nq = S // QC
    qx = q.reshape(B, H, nq, QC, d).transpose(2, 0, 1, 3, 4)
    ix = top_idx.reshape(B, H, nq, QC, n_sel).transpose(2, 0, 1, 3, 4)
    bi = jnp.arange(B)[:, None, None, None]
    hi = jnp.arange(H)[None, :, None, None]
    offs = jnp.arange(BLK)
    bias_h = rel_bias.astype(jnp.float32).T

    def one_chunk(args):
        c, qc, ic = args
        qp = c * QC + jnp.arange(QC)
        qblk = (c * QC) // BLK
        ksel = kb[bi, hi, ic]
        vsel = vb[bi, hi, ic]
        kpos = ic[..., None] * BLK + offs
        dist = qp[None, None, :, None, None] - kpos
        ok = (ic < qblk)[..., None]
        s_sel = (jnp.einsum('bhqd,bhqnkd->bhqnk', qc, ksel)
                 + bias_h[hi[..., None], t5_bucket(jnp.maximum(dist, 0))])
        s_sel = jnp.where(ok, s_sel, NEG_INF).reshape(B, H, QC, n_sel * BLK)
        kown = lax.dynamic_index_in_dim(kb, qblk, axis=2, keepdims=False)
        vown = lax.dynamic_index_in_dim(vb, qblk, axis=2, keepdims=False)
        dist_own = qp[:, None] - (qblk * BLK + offs)[None, :]
        s_own = (jnp.einsum('bhqd,bhkd->bhqk', qc, kown)
                 + bias_h[:, t5_bucket(jnp.maximum(dist_own, 0))][None])
        s_own = jnp.where(dist_own >= 0, s_own, NEG_INF)
        pr = jax.nn.softmax(jnp.concatenate([s_sel, s_own], axis=-1), axis=-1)
        p_sel = pr[..., :n_sel * BLK].reshape(B, H, QC, n_sel, BLK)
        p_own = pr[..., n_sel * BLK:]
        return (jnp.einsum('bhqnk,bhqnkd->bhqd', p_sel, vsel)
                + jnp.einsum('bhqk,bhkd->bhqd', p_own, vown))

    out = lax.map(one_chunk, (jnp.arange(nq), qx, ix))
    return out.transpose(1, 0, 3, 2, 4).reshape(B, S, H * d)


def setup_inputs(seed: int = 0) -> dict:
    key = jax.random.key(seed)
    ks = jax.random.split(key, 12)
    f = jnp.float32
    nrm = jax.random.normal
    return {
        "x": nrm(ks[0], (BATCH, SEQ, D_MODEL), f),
        "p": nrm(ks[1], (DEPTH, BATCH, SEQ, PLE_DIM), f),
        "norm_g": 1.0 + 0.02 * nrm(ks[2], (DEPTH, D_MODEL), f),
        "w_in": nrm(ks[3], (DEPTH, D_MODEL, IN_WIDTH), f) * D_MODEL ** -0.5,
        "ret_norm_g": 1.0 + 0.02 * nrm(ks[4], (DEPTH, RET_WIDTH), f),
        "q_norm_g": 1.0 + 0.02 * nrm(ks[5], (DEPTH, MOBA_HEAD_DIM), f),
        "k_norm_g": 1.0 + 0.02 * nrm(ks[6], (DEPTH, MOBA_HEAD_DIM), f),
        "rel_bias": 0.5 * nrm(ks[7], (N_BUCKETS, MOBA_HEADS), f),
        "w_out": nrm(ks[8], (DEPTH, MIX_WIDTH, D_MODEL), f) * MIX_WIDTH ** -0.5,
        "w_ple": nrm(ks[9], (DEPTH, PLE_DIM, D_MODEL), f) * PLE_DIM ** -0.5,
        "ple_norm_g": 1.0 + 0.02 * nrm(ks[10], (DEPTH, D_MODEL), f),
        "w_ple_gate": nrm(ks[11], (DEPTH, D_MODEL, D_MODEL), f) * D_MODEL ** -0.5,
    }


def reference(x, p, norm_g, w_in, ret_norm_g, q_norm_g, k_norm_g, rel_bias,
              w_out, w_ple, ple_norm_g, w_ple_gate):
    B, S, _ = x.shape
    f32 = jnp.float32
    pos = jnp.arange(S)
    cuts = list(np.cumsum(SPLIT_SIZES)[:-1])

    def heads(t, H):
        return t.astype(f32).reshape(B, S, H, -1)

    for i in range(DEPTH):
        h = rms_norm(x, norm_g[i])
        proj = h @ w_in[i]
        rq, rk, rv, rz, mq, mk, mv, mz = jnp.split(proj, cuts, axis=-1)
        r_out = retention(rotary(heads(rq, RET_HEADS), pos), rotary(heads(rk, RET_HEADS), pos),
                          heads(rv, RET_HEADS), ret_norm_g[i])
        r_out = jax.nn.silu(rz.astype(f32)) * r_out
        q_m = rms_norm(heads(mq, MOBA_HEADS), q_norm_g[i])
        k_m = rms_norm(heads(mk, MOBA_HEADS), k_norm_g[i])
        m_out = moba(q_m, k_m, heads(mv, MOBA_HEADS), rel_bias)
        m_out = jax.nn.silu(mz.astype(f32)) * m_out
        y = jnp.concatenate([r_out, m_out], axis=-1).astype(x.dtype) @ w_out[i]
        x = x + y
        gate = jax.nn.sigmoid((rms_norm(x) @ w_ple_gate[i]).astype(f32))
        e = rms_norm(p[i] @ w_ple[i], ple_norm_g[i]).astype(f32)
        x = x + (gate * e).astype(x.dtype)
    return x
```

```python
import functools
import math

import numpy as np
import jax
import jax.numpy as jnp
from jax import lax
from jax.experimental import pallas as pl
from jax.experimental.pallas import tpu as pltpu

F32 = jnp.float32
BF16 = jnp.bfloat16

HEAD_DIM = 128
RET_HEADS = 4
MOBA_HEADS = 4
GROUP_WIDTH = 512
N_SECTIONS = 8
RET_CHUNK = 128
MOBA_BLOCK = 256
MOBA_TOPK = 3
N_BUCKETS = 32
MAX_DISTANCE = 2048
ROPE_THETA = 10000.0
EPS = 1e-6
NEG_INF = -1e30
MASKED_GATE = -3e38

ROW_TILE = 512
RET_ROWS = 512
VMEM_LIMIT = 56 * 1024 * 1024

_NT = (((1,), (1,)), ((), ()))
_TN = (((0,), (0,)), ((), ()))


def _bucket_thresholds():
    max_exact = N_BUCKETS // 2
    thr = []
    for k in range(N_BUCKETS - max_exact):
        n = max_exact
        while min(max_exact + int(math.log(n / max_exact) / math.log(MAX_DISTANCE / max_exact)
                                  * (N_BUCKETS - max_exact)), N_BUCKETS - 1) < max_exact + k:
            n += 1
        thr.append(n)
    return thr


_BUCKET_THR = _bucket_thresholds()
N_NEAR = next(d for d in range(1, 1 << 20) if (d - 1) * MOBA_BLOCK + 1 >= _BUCKET_THR[-1])
N_BIAS_TILES = N_NEAR + 1


def _bias_kernel(rb_ref, out_ref):
    h = pl.program_id(0)
    n_heads = pl.num_programs(0)
    r = lax.broadcasted_iota(jnp.int32, (MOBA_BLOCK, MOBA_BLOCK), 0)
    c = lax.broadcasted_iota(jnp.int32, (MOBA_BLOCK, MOBA_BLOCK), 1)
    max_exact = N_BUCKETS // 2

    def rb(b):
        return rb_ref[b * n_heads + h]

    for d in range(N_NEAR):
        n = d * MOBA_BLOCK + r - c
        nn = jnp.maximum(n, 0)
        val = jnp.full((MOBA_BLOCK, MOBA_BLOCK), rb(0), F32)
        for b in range(1, max_exact):
            val = jnp.where(nn >= b, rb(b), val)
        for k, thr in enumerate(_BUCKET_THR):
            val = jnp.where(nn >= thr, rb(max_exact + k), val)
        if d == 0:
            val = jnp.where(n >= 0, val, NEG_INF)
        out_ref[0, d] = val
    out_ref[0, N_NEAR] = jnp.full((MOBA_BLOCK, MOBA_BLOCK), rb(N_BUCKETS - 1), F32)


def _bias_tiles(rel_bias):
    n_heads = rel_bias.shape[1]
    return pl.pallas_call(
        _bias_kernel,
        grid=(n_heads,),
        in_specs=[pl.BlockSpec(memory_space=pltpu.SMEM)],
        out_specs=pl.BlockSpec((1, N_BIAS_TILES, MOBA_BLOCK, MOBA_BLOCK), lambda h: (h, 0, 0, 0)),
        out_shape=jax.ShapeDtypeStruct((n_heads, N_BIAS_TILES, MOBA_BLOCK, MOBA_BLOCK), F32),
        name="bias_tiles",
    )(rel_bias.astype(F32).reshape(-1))


def _inproj_kernel(x_ref, g_ref, w_ref, cos_ref, sin_ref, qg_ref, kg_ref, proj_ref, kmean_ref):
    x = x_ref[...]
    ms = jnp.mean(x * x, axis=-1, keepdims=True)
    h = (x * lax.rsqrt(ms + EPS) * g_ref[...]).astype(BF16)
    cos = cos_ref[...]
    sin = sin_ref[...]
    scale = HEAD_DIM ** -0.5
    rows = x.shape[0]

    def head_norm(y, gain):
        return y * lax.rsqrt(jnp.mean(y * y, axis=-1, keepdims=True) + EPS) * gain

    for s in range(N_SECTIONS):
        y = jnp.dot(h, w_ref[:, s * GROUP_WIDTH:(s + 1) * GROUP_WIDTH], preferred_element_type=F32)
        for hh in range(GROUP_WIDTH // HEAD_DIM):
            yh = y[:, hh * HEAD_DIM:(hh + 1) * HEAD_DIM]
            if s in (0, 1):
                yh = yh * cos + pltpu.roll(yh, HEAD_DIM // 2, axis=1) * sin
                if s == 1:
                    yh = yh * scale
            elif s == 4:
                yh = head_norm(yh, qg_ref[...]) * scale
            elif s == 5:
                yh = head_norm(yh, kg_ref[...])
                for blk in range(rows // MOBA_BLOCK):
                    kmean_ref[blk, :, hh * HEAD_DIM:(hh + 1) * HEAD_DIM] = jnp.mean(
                        yh[blk * MOBA_BLOCK:(blk + 1) * MOBA_BLOCK], axis=0, keepdims=True)
            col = s * GROUP_WIDTH + hh * HEAD_DIM
            proj_ref[:, col:col + HEAD_DIM] = yh.astype(BF16)


def _inproj(x2, norm_g, w_in, cos, sin, q_g, k_g, seq):
    m, d_model = x2.shape
    width = w_in.shape[1]
    tm = ROW_TILE
    seq_tiles = seq // tm
    return pl.pallas_call(
        _inproj_kernel,
        grid=(m // tm,),
        in_specs=[
            pl.BlockSpec((tm, d_model), lambda i: (i, 0)),
            pl.BlockSpec((1, d_model), lambda i: (0, 0)),
            pl.BlockSpec((d_model, width), lambda i: (0, 0)),
            pl.BlockSpec((tm, HEAD_DIM), lambda i: (i % seq_tiles, 0)),
            pl.BlockSpec((tm, HEAD_DIM), lambda i: (i % seq_tiles, 0)),
            pl.BlockSpec((1, HEAD_DIM), lambda i: (0, 0)),
            pl.BlockSpec((1, HEAD_DIM), lambda i: (0, 0)),
        ],
        out_specs=[
            pl.BlockSpec((tm, width), lambda i: (i, 0)),
            pl.BlockSpec((tm // MOBA_BLOCK, 1, GROUP_WIDTH), lambda i: (i, 0, 0)),
        ],
        out_shape=[
            jax.ShapeDtypeStruct((m, width), BF16),
            jax.ShapeDtypeStruct((m // MOBA_BLOCK, 1, GROUP_WIDTH), F32),
        ],
        compiler_params=pltpu.CompilerParams(
            dimension_semantics=("arbitrary",), vmem_limit_bytes=VMEM_LIMIT),
        name="in_projection",
    )(x2, norm_g.reshape(1, -1), w_in, cos, sin, q_g.reshape(1, -1), k_g.reshape(1, -1))


def _retention_kernel(q_ref, k_ref, v_ref, z_ref, dec_ref, dq_ref, dk_ref, cd_ref, g_ref,
                      o_ref, state_ref):
    @pl.when(pl.program_id(1) == 0)
    def _():
        state_ref[...] = jnp.zeros_like(state_ref)

    chunk = RET_CHUNK
    for c in range(q_ref.shape[0] // chunk):
        rows = slice(c * chunk, (c + 1) * chunk)
        for h in range(RET_HEADS):
            cols = slice(h * HEAD_DIM, (h + 1) * HEAD_DIM)
            q = q_ref[rows, cols]
            k = k_ref[rows, cols]
            v = v_ref[rows, cols]
            s = lax.dot_general(q, k, _NT, preferred_element_type=F32) * dec_ref[h]
            inner = jnp.dot(s.astype(BF16), v, preferred_element_type=F32)
            state = state_ref[h]
            cross = jnp.dot(q, state.astype(BF16), preferred_element_type=F32) * dq_ref[h]
            o = inner + cross
            v_dec = (v.astype(F32) * dk_ref[h]).astype(BF16)
            kv = lax.dot_general(k, v_dec, _TN, preferred_element_type=F32)
            state_ref[h] = state * cd_ref[h] + kv
            mu = jnp.mean(o, axis=-1, keepdims=True)
            oc = o - mu
            var = jnp.mean(oc * oc, axis=-1, keepdims=True)
            on = oc * lax.rsqrt(var + EPS) * g_ref[:, cols]
            z = z_ref[rows, cols].astype(F32)
            o_ref[rows, cols] = (z * jax.nn.sigmoid(z) * on).astype(BF16)


def _retention_tables():
    c = RET_CHUNK
    log_g = np.log1p(-np.power(2.0, -5.0 - np.arange(RET_HEADS, dtype=np.float64)))
    j = np.arange(c, dtype=np.float64)
    rel = j[:, None] - j[None, :]
    decay = np.where(rel >= 0, np.exp(log_g[:, None, None] * np.maximum(rel, 0.0)[None]), 0.0)
    dq = np.exp(log_g[:, None] * (j + 1.0)[None])
    dk = np.exp(log_g[:, None] * (c - 1 - j)[None])
    cd = np.exp(log_g * c)
    wide = lambda t: np.broadcast_to(t[:, :, None], (RET_HEADS, c, HEAD_DIM))
    return (jnp.asarray(decay, F32), jnp.asarray(wide(dq), F32), jnp.asarray(wide(dk), F32),
            jnp.asarray(np.broadcast_to(cd[:, None, None], (RET_HEADS, 1, HEAD_DIM)), F32))


def _retention(proj, gain, batch, seq):
    m = proj.shape[0]
    ts = RET_ROWS
    steps = seq // ts
    dec, dq, dk, cd = _retention_tables()
    section = lambda sec: pl.BlockSpec((ts, GROUP_WIDTH), lambda b, i: (b * steps + i, sec))
    full = lambda a: pl.BlockSpec(a.shape, lambda b, i: (0,) * a.ndim)
    return pl.pallas_call(
        _retention_kernel,
        grid=(batch, steps),
        in_specs=[section(0), section(1), section(2), section(3),
                  full(dec), full(dq), full(dk), full(cd),
                  pl.BlockSpec((1, GROUP_WIDTH), lambda b, i: (0, 0))],
        out_specs=pl.BlockSpec((ts, GROUP_WIDTH), lambda b, i: (b * steps + i, 0)),
        out_shape=jax.ShapeDtypeStruct((m, GROUP_WIDTH), BF16),
        scratch_shapes=[pltpu.VMEM((RET_HEADS, HEAD_DIM, HEAD_DIM), F32)],
        compiler_params=pltpu.CompilerParams(
            dimension_semantics=("arbitrary", "arbitrary"), vmem_limit_bytes=VMEM_LIMIT),
        name="retention",
    )(proj, proj, proj, proj, dec, dq, dk, cd, gain.reshape(1, -1))


def _moba_kernel(q_ref, k_ref, v_ref, z_ref, km_ref, bias_ref, o_ref, m_scr, l_scr, acc_scr):
    blk = MOBA_BLOCK
    qi = pl.program_id(2)
    q = q_ref[...]

    km = km_ref[0]
    km_hi = km.astype(BF16)
    rest = km - km_hi.astype(F32)
    km_mid = rest.astype(BF16)
    km_lo = (rest - km_mid.astype(F32)).astype(BF16)
    gate = (lax.dot_general(q, km_hi, _NT, preferred_element_type=F32)
            + lax.dot_general(q, km_mid, _NT, preferred_element_type=F32)
            + lax.dot_general(q, km_lo, _NT, preferred_element_type=F32))

    lane = lax.broadcasted_iota(jnp.int32, gate.shape, 1)
    lane_f = lane.astype(F32)
    past = lane < qi
    g = jnp.where(past, gate, MASKED_GATE)
    sel_bias = jnp.full(gate.shape, NEG_INF, F32)
    for _ in range(MOBA_TOPK):
        top = jnp.max(g, axis=1, keepdims=True)
        first = jnp.min(jnp.where(g == top, lane_f, 1e9), axis=1, keepdims=True)
        pick = lane_f == first
        sel_bias = jnp.where(pick, 0.0, sel_bias)
        g = jnp.where(pick, MASKED_GATE, g)
    sel_bias = jnp.where(past, sel_bias, NEG_INF)
    sel_bias = jnp.where(lane == qi, 0.0, sel_bias)
    q_aug = jnp.concatenate([q, sel_bias.astype(BF16)], axis=1)

    m_scr[...] = jnp.full(m_scr.shape, NEG_INF, F32)
    l_scr[...] = jnp.zeros(l_scr.shape, F32)
    acc_scr[...] = jnp.zeros(acc_scr.shape, F32)
    key_lane = lax.broadcasted_iota(jnp.int32, (blk, HEAD_DIM), 1)

    def body(t, carry):
        j = qi - t
        start = pl.multiple_of(j * blk, blk)
        kj = k_ref[pl.ds(start, blk), :]
        vj = v_ref[pl.ds(start, blk), :]
        onehot = jnp.where(key_lane == j, 1.0, 0.0).astype(BF16)
        k_aug = jnp.concatenate([kj, onehot], axis=1)
        s = lax.dot_general(q_aug, k_aug, _NT, preferred_element_type=F32)
        s = s + bias_ref[0, jnp.minimum(t, N_NEAR)]
        m_prev = m_scr[...]
        m_next = jnp.maximum(m_prev, jnp.max(s, axis=1, keepdims=True))
        alpha = jnp.exp(m_prev - m_next)
        p = jnp.exp(s - pltpu.repeat(m_next, blk // HEAD_DIM, 1))
        l_scr[...] = alpha * l_scr[...] + jnp.sum(p, axis=1, keepdims=True)
        acc_scr[...] = alpha * acc_scr[...] + jnp.dot(p.astype(BF16), vj,
                                                      preferred_element_type=F32)
        m_scr[...] = m_next
        return carry

    lax.fori_loop(0, qi + 1, body, 0)
    z = z_ref[...].astype(F32)
    o_ref[...] = (z * jax.nn.sigmoid(z) * (acc_scr[...] / l_scr[...])).astype(BF16)


def _moba(proj, kmean, bias_tiles, batch, seq):
    m = proj.shape[0]
    blk = MOBA_BLOCK
    nq = seq // blk
    qsec, ksec, vsec, zsec = (sec * (GROUP_WIDTH // HEAD_DIM) for sec in (4, 5, 6, 7))
    tile = lambda sec: pl.BlockSpec((blk, HEAD_DIM), lambda b, h, i: (b * nq + i, sec + h))
    whole = lambda sec: pl.BlockSpec((seq, HEAD_DIM), lambda b, h, i: (b, sec + h))
    return pl.pallas_call(
        _moba_kernel,
        grid=(batch, MOBA_HEADS, nq),
        in_specs=[tile(qsec), whole(ksec), whole(vsec), tile(zsec),
                  pl.BlockSpec((1, kmean.shape[1], HEAD_DIM), lambda b, h, i: (b, 0, h)),
                  pl.BlockSpec((1, N_BIAS_TILES, blk, blk), lambda b, h, i: (h, 0, 0, 0))],
        out_specs=pl.BlockSpec((blk, HEAD_DIM), lambda b, h, i: (b * nq + i, h)),
        out_shape=jax.ShapeDtypeStruct((m, GROUP_WIDTH), BF16),
        scratch_shapes=[pltpu.VMEM((blk, HEAD_DIM), F32)] * 3,
        compiler_params=pltpu.CompilerParams(
            dimension_semantics=("arbitrary", "arbitrary", "arbitrary"),
            vmem_limit_bytes=VMEM_LIMIT),
        name="moba",
    )(proj, proj, proj, proj, kmean, bias_tiles)


def _outproj_kernel(x_ref, r_ref, a_ref, wo_ref, p_ref, wp_ref, pg_ref, wg_ref, o_ref):
    def unit_rms(t):
        return t * lax.rsqrt(jnp.mean(t * t, axis=-1, keepdims=True) + EPS)

    y = (jnp.dot(r_ref[...], wo_ref[:GROUP_WIDTH, :], preferred_element_type=F32)
         + jnp.dot(a_ref[...], wo_ref[GROUP_WIDTH:, :], preferred_element_type=F32))
    x1 = x_ref[...] + y
    gate = jax.nn.sigmoid(jnp.dot(unit_rms(x1).astype(BF16), wg_ref[...],
                                  preferred_element_type=F32))
    e = unit_rms(jnp.dot(p_ref[...].astype(BF16), wp_ref[...], preferred_element_type=F32))
    o_ref[...] = x1 + gate * (e * pg_ref[...])


def _outproj(x2, r_out, m_out, w_out, p2, w_ple, ple_g, w_gate):
    m, d_model = x2.shape
    tm = ROW_TILE
    rows = lambda a: pl.BlockSpec((tm, a.shape[1]), lambda i: (i, 0))
    full = lambda a: pl.BlockSpec(a.shape, lambda i: (0, 0))
    ple_g = ple_g.reshape(1, -1)
    return pl.pallas_call(
        _outproj_kernel,
        grid=(m // tm,),
        in_specs=[rows(x2), rows(r_out), rows(m_out), full(w_out), rows(p2), full(w_ple),
                  full(ple_g), full(w_gate)],
        out_specs=pl.BlockSpec((tm, d_model), lambda i: (i, 0)),
        out_shape=jax.ShapeDtypeStruct((m, d_model), F32),
        compiler_params=pltpu.CompilerParams(
            dimension_semantics=("arbitrary",), vmem_limit_bytes=VMEM_LIMIT),
        name="out_projection",
    )(x2, r_out, m_out, w_out, p2, w_ple, ple_g, w_gate)


def _rotary_tables(seq):
    half = HEAD_DIM // 2
    inv = ROPE_THETA ** (-jnp.arange(half, dtype=F32) / half)
    ang = jnp.arange(seq, dtype=F32)[:, None] * inv[None, :]
    cos, sin = jnp.cos(ang), jnp.sin(ang)
    return jnp.concatenate([cos, cos], axis=1), jnp.concatenate([-sin, sin], axis=1)


def kernel(x, p, norm_g, w_in, ret_norm_g, q_norm_g, k_norm_g, rel_bias, w_out, w_ple, ple_norm_g, w_ple_gate):
    batch, seq, d_model = x.shape
    depth = w_in.shape[0]
    m = batch * seq
    assert w_in.shape[2] == N_SECTIONS * GROUP_WIDTH and w_out.shape[1] == 2 * GROUP_WIDTH
    assert rel_bias.shape == (N_BUCKETS, MOBA_HEADS)
    assert seq % ROW_TILE == 0 and seq % RET_ROWS == 0 and ROW_TILE % MOBA_BLOCK == 0
    assert RET_ROWS % RET_CHUNK == 0 and seq // MOBA_BLOCK <= HEAD_DIM

    cos, sin = _rotary_tables(seq)
    bias_tiles = _bias_tiles(rel_bias)
    x2 = x.reshape(m, d_model)
    nb = seq // MOBA_BLOCK
    for i in range(depth):
        proj, kmean = _inproj(x2, norm_g[i], w_in[i].astype(BF16), cos, sin,
                              q_norm_g[i], k_norm_g[i], seq)
        kmean = jnp.pad(kmean.reshape(batch, nb, GROUP_WIDTH), ((0, 0), (0, HEAD_DIM - nb), (0, 0)))
        r_out = _retention(proj, ret_norm_g[i], batch, seq)
        m_out = _moba(proj, kmean, bias_tiles, batch, seq)
        x2 = _outproj(x2, r_out, m_out, w_out[i].astype(BF16), p[i].reshape(m, -1),
                      w_ple[i].astype(BF16), ple_norm_g[i], w_ple_gate[i].astype(BF16))
    return x2.reshape(batch, seq, d_model)
```

```python
import functools
import math

import numpy as np
import jax
import jax.numpy as jnp
from jax import lax
from jax.experimental import pallas as pl
from jax.experimental.pallas import tpu as pltpu

F32 = jnp.float32
BF16 = jnp.bfloat16

HEAD_DIM = 128
RET_HEADS = 4
MOBA_HEADS = 4
GROUP_WIDTH = 512
N_SECTIONS = 8
RET_CHUNK = 128
MOBA_BLOCK = 256
MOBA_TOPK = 3
N_BUCKETS = 32
MAX_DISTANCE = 2048
ROPE_THETA = 10000.0
EPS = 1e-6
NEG_INF = -1e30
MASKED_GATE = -3e38
LOG2E = math.log2(math.e)
SEL_LANES = HEAD_DIM // 2

ROW_TILE = 512
RET_ROWS = 512
VMEM_LIMIT = 56 * 1024 * 1024

_NT = (((1,), (1,)), ((), ()))
_TN = (((0,), (0,)), ((), ()))


def _bucket_thresholds():
    max_exact = N_BUCKETS // 2
    thr = []
    for k in range(N_BUCKETS - max_exact):
        n = max_exact
        while min(max_exact + int(math.log(n / max_exact) / math.log(MAX_DISTANCE / max_exact)
                                  * (N_BUCKETS - max_exact)), N_BUCKETS - 1) < max_exact + k:
            n += 1
        thr.append(n)
    return thr


_BUCKET_THR = _bucket_thresholds()
N_NEAR = next(d for d in range(1, 1 << 20) if (d - 1) * MOBA_BLOCK + 1 >= _BUCKET_THR[-1])
N_NEAR_PAIRS = (N_NEAR + 1) // 2
N_NEAR_BLOCKS = 2 * N_NEAR_PAIRS


def _bias_kernel(rb_ref, out_ref):
    h = pl.program_id(0)
    n_heads = pl.num_programs(0)
    r = lax.broadcasted_iota(jnp.int32, (MOBA_BLOCK, MOBA_BLOCK), 0)
    c = lax.broadcasted_iota(jnp.int32, (MOBA_BLOCK, MOBA_BLOCK), 1)
    max_exact = N_BUCKETS // 2

    def rb(b):
        return rb_ref[b * n_heads + h] * LOG2E

    for d in range(N_NEAR_BLOCKS):
        n = d * MOBA_BLOCK + r - c
        nn = jnp.maximum(n, 0)
        val = jnp.full((MOBA_BLOCK, MOBA_BLOCK), rb(0), F32)
        for b in range(1, max_exact):
            val = jnp.where(nn >= b, rb(b), val)
        for k, thr in enumerate(_BUCKET_THR):
            val = jnp.where(nn >= thr, rb(max_exact + k), val)
        if d == 0:
            val = jnp.where(n >= 0, val, NEG_INF)
        half = 1 - d % 2
        out_ref[0, d // 2, :, half * MOBA_BLOCK:(half + 1) * MOBA_BLOCK] = val


def _bias_tiles(rel_bias):
    n_heads = rel_bias.shape[1]
    return pl.pallas_call(
        _bias_kernel,
        grid=(n_heads,),
        in_specs=[pl.BlockSpec(memory_space=pltpu.SMEM)],
        out_specs=pl.BlockSpec((1, N_NEAR_PAIRS, MOBA_BLOCK, 2 * MOBA_BLOCK), lambda h: (h, 0, 0, 0)),
        out_shape=jax.ShapeDtypeStruct((n_heads, N_NEAR_PAIRS, MOBA_BLOCK, 2 * MOBA_BLOCK), F32),
        name="bias_tiles",
    )(rel_bias.astype(F32).reshape(-1))


def _inproj_kernel(x_ref, g_ref, w_ref, cos_ref, sin_ref, qg_ref, kg_ref, proj_ref, kmean_ref):
    x = x_ref[...]
    ms = jnp.mean(x * x, axis=-1, keepdims=True)
    h = (x * lax.rsqrt(ms + EPS) * g_ref[...]).astype(BF16)
    cos = cos_ref[...]
    sin = sin_ref[...]
    scale = HEAD_DIM ** -0.5
    rows = x.shape[0]

    def head_norm(y, gain):
        return y * lax.rsqrt(jnp.mean(y * y, axis=-1, keepdims=True) + EPS) * gain

    for s in range(N_SECTIONS):
        y = jnp.dot(h, w_ref[:, s * GROUP_WIDTH:(s + 1) * GROUP_WIDTH], preferred_element_type=F32)
        for hh in range(GROUP_WIDTH // HEAD_DIM):
            yh = y[:, hh * HEAD_DIM:(hh + 1) * HEAD_DIM]
            if s in (0, 1):
                yh = yh * cos + pltpu.roll(yh, HEAD_DIM // 2, axis=1) * sin
                if s == 1:
                    yh = yh * scale
            elif s == 4:
                yh = head_norm(yh, qg_ref[...]) * (scale * LOG2E)
            elif s == 5:
                yh = head_norm(yh, kg_ref[...])
                for blk in range(rows // MOBA_BLOCK):
                    kmean_ref[blk, :, hh * HEAD_DIM:(hh + 1) * HEAD_DIM] = jnp.mean(
                        yh[blk * MOBA_BLOCK:(blk + 1) * MOBA_BLOCK], axis=0, keepdims=True)
            col = s * GROUP_WIDTH + hh * HEAD_DIM
            proj_ref[:, col:col + HEAD_DIM] = yh.astype(BF16)


def _inproj(x2, norm_g, w_in, cos, sin, q_g, k_g, seq):
    m, d_model = x2.shape
    width = w_in.shape[1]
    tm = ROW_TILE
    seq_tiles = seq // tm
    return pl.pallas_call(
        _inproj_kernel,
        grid=(m // tm,),
        in_specs=[
            pl.BlockSpec((tm, d_model), lambda i: (i, 0)),
            pl.BlockSpec((1, d_model), lambda i: (0, 0)),
            pl.BlockSpec((d_model, width), lambda i: (0, 0)),
            pl.BlockSpec((tm, HEAD_DIM), lambda i: (i % seq_tiles, 0)),
            pl.BlockSpec((tm, HEAD_DIM), lambda i: (i % seq_tiles, 0)),
            pl.BlockSpec((1, HEAD_DIM), lambda i: (0, 0)),
            pl.BlockSpec((1, HEAD_DIM), lambda i: (0, 0)),
        ],
        out_specs=[
            pl.BlockSpec((tm, width), lambda i: (i, 0)),
            pl.BlockSpec((tm // MOBA_BLOCK, 1, GROUP_WIDTH), lambda i: (i, 0, 0)),
        ],
        out_shape=[
            jax.ShapeDtypeStruct((m, width), BF16),
            jax.ShapeDtypeStruct((m // MOBA_BLOCK, 1, GROUP_WIDTH), F32),
        ],
        compiler_params=pltpu.CompilerParams(
            dimension_semantics=("arbitrary",), vmem_limit_bytes=VMEM_LIMIT),
        name="in_projection",
    )(x2, norm_g.reshape(1, -1), w_in, cos, sin, q_g.reshape(1, -1), k_g.reshape(1, -1))


def _retention_kernel(q_ref, k_ref, v_ref, z_ref, dec_ref, dq_ref, dk_ref, cd_ref, g_ref,
                      o_ref, state_ref):
    @pl.when(pl.program_id(1) == 0)
    def _():
        state_ref[...] = jnp.zeros_like(state_ref)

    chunk = RET_CHUNK
    for c in range(q_ref.shape[0] // chunk):
        rows = slice(c * chunk, (c + 1) * chunk)
        for h in range(RET_HEADS):
            cols = slice(h * HEAD_DIM, (h + 1) * HEAD_DIM)
            q = q_ref[rows, cols]
            k = k_ref[rows, cols]
            v = v_ref[rows, cols]
            s = lax.dot_general(q, k, _NT, preferred_element_type=F32) * dec_ref[h]
            inner = jnp.dot(s.astype(BF16), v, preferred_element_type=F32)
            state = state_ref[h]
            cross = jnp.dot(q, state.astype(BF16), preferred_element_type=F32) * dq_ref[h]
            o = inner + cross
            v_dec = (v.astype(F32) * dk_ref[h]).astype(BF16)
            kv = lax.dot_general(k, v_dec, _TN, preferred_element_type=F32)
            state_ref[h] = state * cd_ref[h] + kv
            mu = jnp.mean(o, axis=-1, keepdims=True)
            oc = o - mu
            var = jnp.mean(oc * oc, axis=-1, keepdims=True)
            on = oc * lax.rsqrt(var + EPS) * g_ref[:, cols]
            z = z_ref[rows, cols].astype(F32)
            o_ref[rows, cols] = (z * jax.nn.sigmoid(z) * on).astype(BF16)


def _retention_tables():
    c = RET_CHUNK
    log_g = np.log1p(-np.power(2.0, -5.0 - np.arange(RET_HEADS, dtype=np.float64)))
    j = np.arange(c, dtype=np.float64)
    rel = j[:, None] - j[None, :]
    decay = np.where(rel >= 0, np.exp(log_g[:, None, None] * np.maximum(rel, 0.0)[None]), 0.0)
    dq = np.exp(log_g[:, None] * (j + 1.0)[None])
    dk = np.exp(log_g[:, None] * (c - 1 - j)[None])
    cd = np.exp(log_g * c)
    wide = lambda t: np.broadcast_to(t[:, :, None], (RET_HEADS, c, HEAD_DIM))
    return (jnp.asarray(decay, F32), jnp.asarray(wide(dq), F32), jnp.asarray(wide(dk), F32),
            jnp.asarray(np.broadcast_to(cd[:, None, None], (RET_HEADS, 1, HEAD_DIM)), F32))


def _retention(proj, gain, batch, seq):
    m = proj.shape[0]
    ts = RET_ROWS
    steps = seq // ts
    dec, dq, dk, cd = _retention_tables()
    section = lambda sec: pl.BlockSpec((ts, GROUP_WIDTH), lambda b, i: (b * steps + i, sec))
    full = lambda a: pl.BlockSpec(a.shape, lambda b, i: (0,) * a.ndim)
    return pl.pallas_call(
        _retention_kernel,
        grid=(batch, steps),
        in_specs=[section(0), section(1), section(2), section(3),
                  full(dec), full(dq), full(dk), full(cd),
                  pl.BlockSpec((1, GROUP_WIDTH), lambda b, i: (0, 0))],
        out_specs=pl.BlockSpec((ts, GROUP_WIDTH), lambda b, i: (b * steps + i, 0)),
        out_shape=jax.ShapeDtypeStruct((m, GROUP_WIDTH), BF16),
        scratch_shapes=[pltpu.VMEM((RET_HEADS, HEAD_DIM, HEAD_DIM), F32)],
        compiler_params=pltpu.CompilerParams(
            dimension_semantics=("arbitrary", "arbitrary"), vmem_limit_bytes=VMEM_LIMIT),
        name="retention",
    )(proj, proj, proj, proj, dec, dq, dk, cd, gain.reshape(1, -1))


def _moba_kernel(rb_ref, q_ref, k_ref, v_ref, z_ref, km_ref, oh_ref, bias_ref, o_ref,
                 qaug_scr, m_scr, acc_scr):
    blk = MOBA_BLOCK
    qi = pl.program_id(1)
    n_heads = MOBA_HEADS
    lane = lax.broadcasted_iota(jnp.int32, (blk, HEAD_DIM), 1)
    blk_id = lane & (SEL_LANES - 1)
    blk_f = blk_id.astype(F32)
    low_half = lane < SEL_LANES
    past = blk_id < qi
    candidate = past & low_half
    far = (qi - blk_id) >= N_NEAR_BLOCKS
    off_val = jnp.where(low_half, NEG_INF, 0.0)

    for h in range(n_heads):
        cols = slice(h * HEAD_DIM, (h + 1) * HEAD_DIM)
        q = q_ref[:, cols]
        km = km_ref[0, :, cols]
        km_hi = km.astype(BF16)
        rest = km - km_hi.astype(F32)
        km_mid = rest.astype(BF16)
        km_lo = (rest - km_mid.astype(F32)).astype(BF16)
        gate = (lax.dot_general(q, km_hi, _NT, preferred_element_type=F32)
                + lax.dot_general(q, km_mid, _NT, preferred_element_type=F32)
                + lax.dot_general(q, km_lo, _NT, preferred_element_type=F32))
        g = jnp.where(candidate, gate, MASKED_GATE)
        sel = jnp.zeros(gate.shape, F32)
        for _ in range(MOBA_TOPK):
            top = jnp.max(g, axis=1, keepdims=True)
            first = jnp.min(jnp.where(g == top, blk_f, 1e9), axis=1, keepdims=True)
            pick = blk_f == first
            sel = jnp.where(pick, 1.0, sel)
            g = jnp.where(pick, MASKED_GATE, g)
        sel = jnp.where(past, sel, 0.0)
        far_c = jnp.full(gate.shape, rb_ref[(N_BUCKETS - 1) * n_heads + h] * LOG2E, F32)
        far_hi = far_c.astype(BF16).astype(F32)
        on_val = jnp.where(far, jnp.where(low_half, far_hi, far_c - far_hi), 0.0)
        mask_row = jnp.where(sel > 0.0, on_val, off_val)
        mask_row = jnp.where(lane == qi, 0.0, mask_row)
        qaug_scr[h] = jnp.concatenate([q, mask_row.astype(BF16)], axis=1)

    m_scr[...] = jnp.full(m_scr.shape, NEG_INF, F32)
    acc_scr[...] = jnp.zeros(acc_scr.shape, F32)
    ones = jnp.ones((2 * blk, HEAD_DIM), BF16)
    n_blocks = k_ref.shape[0] // blk

    def step(pair, first, with_bias):
        j_new = qi - (first + 2 * pair)
        j_old = j_new - 1
        new = pl.multiple_of(j_new * blk, blk)
        old = pl.multiple_of(jnp.maximum(j_old, 0) * blk, blk)
        old_id = pl.multiple_of(jnp.where(j_old < 0, n_blocks, j_old) * blk, blk)
        onehot = jnp.concatenate([oh_ref[pl.ds(old_id, blk), :], oh_ref[pl.ds(new, blk), :]], axis=0)
        for h in range(n_heads):
            cols = slice(h * HEAD_DIM, (h + 1) * HEAD_DIM)
            k_pair = jnp.concatenate([k_ref[pl.ds(old, blk), cols], k_ref[pl.ds(new, blk), cols]], axis=0)
            v_pair = jnp.concatenate([v_ref[pl.ds(old, blk), cols], v_ref[pl.ds(new, blk), cols]], axis=0)
            k_aug = jnp.concatenate([k_pair, onehot], axis=1)
            v_aug = jnp.concatenate([v_pair, ones], axis=1)
            s = lax.dot_general(qaug_scr[h], k_aug, _NT, preferred_element_type=F32)
            if with_bias:
                s = s + bias_ref[h, pair]
            m_prev = m_scr[h]
            m_next = jnp.maximum(m_prev, jnp.max(s, axis=1, keepdims=True))
            alpha = jnp.exp2(m_prev - m_next)
            p = jnp.exp2(s - pltpu.repeat(m_next, 2 * blk // HEAD_DIM, 1))
            acc_scr[h] = (pltpu.repeat(alpha, 2, 1) * acc_scr[h]
                          + jnp.dot(p.astype(BF16), v_aug, preferred_element_type=F32))
            m_scr[h] = m_next

    def near_body(pair, carry):
        step(pair, 0, True)
        return carry

    def far_body(pair, carry):
        step(pair, N_NEAR_BLOCKS, False)
        return carry

    n_near = jnp.minimum(qi + 1, N_NEAR_BLOCKS)
    n_far = jnp.maximum(qi + 1 - N_NEAR_BLOCKS, 0)
    lax.fori_loop(0, lax.shift_right_logical(n_near + 1, 1), near_body, 0)
    lax.fori_loop(0, lax.shift_right_logical(n_far + 1, 1), far_body, 0)
    for h in range(n_heads):
        cols = slice(h * HEAD_DIM, (h + 1) * HEAD_DIM)
        z = z_ref[:, cols].astype(F32)
        out = acc_scr[h, :, :HEAD_DIM] / acc_scr[h, :, HEAD_DIM:]
        o_ref[:, cols] = (z * jax.nn.sigmoid(z) * out).astype(BF16)


def _moba(proj, kmean, bias_tiles, rel_bias, batch, seq):
    m = proj.shape[0]
    blk = MOBA_BLOCK
    nq = seq // blk
    row_blk = lax.broadcasted_iota(jnp.int32, (seq + blk, HEAD_DIM), 0) // blk
    lane = lax.broadcasted_iota(jnp.int32, (seq + blk, HEAD_DIM), 1)
    onehot = ((lane & (SEL_LANES - 1)) == row_blk).astype(BF16)
    tile = lambda sec: pl.BlockSpec((blk, GROUP_WIDTH), lambda b, i: (b * nq + i, sec))
    whole = lambda sec: pl.BlockSpec((seq, GROUP_WIDTH), lambda b, i: (b, sec))
    return pl.pallas_call(
        _moba_kernel,
        grid=(batch, nq),
        in_specs=[pl.BlockSpec(memory_space=pltpu.SMEM),
                  tile(4), whole(5), whole(6), tile(7),
                  pl.BlockSpec((1, HEAD_DIM, GROUP_WIDTH), lambda b, i: (b, 0, 0)),
                  pl.BlockSpec(onehot.shape, lambda b, i: (0, 0), pipeline_mode=pl.Buffered(1)),
                  pl.BlockSpec(bias_tiles.shape, lambda b, i: (0, 0, 0, 0),
                               pipeline_mode=pl.Buffered(1))],
        out_specs=pl.BlockSpec((blk, GROUP_WIDTH), lambda b, i: (b * nq + i, 0)),
        out_shape=jax.ShapeDtypeStruct((m, GROUP_WIDTH), BF16),
        scratch_shapes=[pltpu.VMEM((MOBA_HEADS, blk, 2 * HEAD_DIM), BF16),
                        pltpu.VMEM((MOBA_HEADS, blk, HEAD_DIM), F32),
                        pltpu.VMEM((MOBA_HEADS, blk, 2 * HEAD_DIM), F32)],
        compiler_params=pltpu.CompilerParams(
            dimension_semantics=("arbitrary", "arbitrary"), vmem_limit_bytes=VMEM_LIMIT),
        name="moba",
    )(rel_bias.astype(F32).reshape(-1), proj, proj, proj, proj, kmean, onehot, bias_tiles)


def _outproj_kernel(x_ref, r_ref, a_ref, wo_ref, p_ref, wp_ref, pg_ref, wg_ref, o_ref):
    def unit_rms(t):
        return t * lax.rsqrt(jnp.mean(t * t, axis=-1, keepdims=True) + EPS)

    y = (jnp.dot(r_ref[...], wo_ref[:GROUP_WIDTH, :], preferred_element_type=F32)
         + jnp.dot(a_ref[...], wo_ref[GROUP_WIDTH:, :], preferred_element_type=F32))
    x1 = x_ref[...] + y
    gate = jax.nn.sigmoid(jnp.dot(unit_rms(x1).astype(BF16), wg_ref[...],
                                  preferred_element_type=F32))
    e = unit_rms(jnp.dot(p_ref[...].astype(BF16), wp_ref[...], preferred_element_type=F32))
    o_ref[...] = x1 + gate * (e * pg_ref[...])


def _outproj(x2, r_out, m_out, w_out, p2, w_ple, ple_g, w_gate):
    m, d_model = x2.shape
    tm = ROW_TILE
    rows = lambda a: pl.BlockSpec((tm, a.shape[1]), lambda i: (i, 0))
    full = lambda a: pl.BlockSpec(a.shape, lambda i: (0, 0))
    ple_g = ple_g.reshape(1, -1)
    return pl.pallas_call(
        _outproj_kernel,
        grid=(m // tm,),
        in_specs=[rows(x2), rows(r_out), rows(m_out), full(w_out), rows(p2), full(w_ple),
                  full(ple_g), full(w_gate)],
        out_specs=pl.BlockSpec((tm, d_model), lambda i: (i, 0)),
        out_shape=jax.ShapeDtypeStruct((m, d_model), F32),
        compiler_params=pltpu.CompilerParams(
            dimension_semantics=("arbitrary",), vmem_limit_bytes=VMEM_LIMIT),
        name="out_projection",
    )(x2, r_out, m_out, w_out, p2, w_ple, ple_g, w_gate)


def _rotary_tables(seq):
    half = HEAD_DIM // 2
    inv = ROPE_THETA ** (-jnp.arange(half, dtype=F32) / half)
    ang = jnp.arange(seq, dtype=F32)[:, None] * inv[None, :]
    cos, sin = jnp.cos(ang), jnp.sin(ang)
    return jnp.concatenate([cos, cos], axis=1), jnp.concatenate([-sin, sin], axis=1)


def kernel(x, p, norm_g, w_in, ret_norm_g, q_norm_g, k_norm_g, rel_bias, w_out, w_ple, ple_norm_g, w_ple_gate):
    batch, seq, d_model = x.shape
    depth = w_in.shape[0]
    m = batch * seq
    assert w_in.shape[2] == N_SECTIONS * GROUP_WIDTH and w_out.shape[1] == 2 * GROUP_WIDTH
    assert rel_bias.shape == (N_BUCKETS, MOBA_HEADS)
    assert seq % ROW_TILE == 0 and seq % RET_ROWS == 0 and ROW_TILE % MOBA_BLOCK == 0
    assert RET_ROWS % RET_CHUNK == 0 and seq // MOBA_BLOCK < SEL_LANES

    cos, sin = _rotary_tables(seq)
    bias_tiles = _bias_tiles(rel_bias)
    x2 = x.reshape(m, d_model)
    nb = seq // MOBA_BLOCK
    for i in range(depth):
        proj, kmean = _inproj(x2, norm_g[i], w_in[i].astype(BF16), cos, sin,
                              q_norm_g[i], k_norm_g[i], seq)
        kmean = jnp.pad(kmean.reshape(batch, nb, GROUP_WIDTH), ((0, 0), (0, SEL_LANES - nb), (0, 0)))
        kmean = jnp.concatenate([kmean, kmean], axis=1)
        r_out = _retention(proj, ret_norm_g[i], batch, seq)
        m_out = _moba(proj, kmean, bias_tiles, rel_bias, batch, seq)
        x2 = _outproj(x2, r_out, m_out, w_out[i].astype(BF16), p[i].reshape(m, -1),
                      w_ple[i].astype(BF16), ple_norm_g[i], w_ple_gate[i].astype(BF16))
    return x2.reshape(batch, seq, d_model)
```

```python
import functools
import math

import numpy as np
import jax
import jax.numpy as jnp
from jax import lax
from jax.experimental import pallas as pl
from jax.experimental.pallas import tpu as pltpu

F32 = jnp.float32
BF16 = jnp.bfloat16

HEAD_DIM = 128
RET_HEADS = 4
MOBA_HEADS = 4
GROUP_WIDTH = 512
N_SECTIONS = 8
RET_CHUNK = 128
MOBA_BLOCK = 256
MOBA_TOPK = 3
N_BUCKETS = 32
MAX_DISTANCE = 2048
ROPE_THETA = 10000.0
EPS = 1e-6
NEG_INF = -1e30
MASKED_GATE = -3e38
LOG2E = math.log2(math.e)
SEL_LANES = HEAD_DIM // 2

ROW_TILE = 512
RET_ROWS = 512
VMEM_LIMIT = 56 * 1024 * 1024

_NT = (((1,), (1,)), ((), ()))
_TN = (((0,), (0,)), ((), ()))


def _bucket_thresholds():
    max_exact = N_BUCKETS // 2
    thr = []
    for k in range(N_BUCKETS - max_exact):
        n = max_exact
        while min(max_exact + int(math.log(n / max_exact) / math.log(MAX_DISTANCE / max_exact)
                                  * (N_BUCKETS - max_exact)), N_BUCKETS - 1) < max_exact + k:
            n += 1
        thr.append(n)
    return thr


_BUCKET_THR = _bucket_thresholds()
N_NEAR = next(d for d in range(1, 1 << 20) if (d - 1) * MOBA_BLOCK + 1 >= _BUCKET_THR[-1])
N_NEAR_PAIRS = (N_NEAR + 1) // 2
N_NEAR_BLOCKS = 2 * N_NEAR_PAIRS


def _bias_kernel(rb_ref, out_ref):
    h = pl.program_id(0)
    n_heads = pl.num_programs(0)
    r = lax.broadcasted_iota(jnp.int32, (MOBA_BLOCK, MOBA_BLOCK), 0)
    c = lax.broadcasted_iota(jnp.int32, (MOBA_BLOCK, MOBA_BLOCK), 1)
    max_exact = N_BUCKETS // 2

    def rb(b):
        return rb_ref[b * n_heads + h] * LOG2E

    for d in range(N_NEAR_BLOCKS):
        n = d * MOBA_BLOCK + r - c
        nn = jnp.maximum(n, 0)
        val = jnp.full((MOBA_BLOCK, MOBA_BLOCK), rb(0), F32)
        for b in range(1, max_exact):
            val = jnp.where(nn >= b, rb(b), val)
        for k, thr in enumerate(_BUCKET_THR):
            val = jnp.where(nn >= thr, rb(max_exact + k), val)
        if d == 0:
            val = jnp.where(n >= 0, val, NEG_INF)
        half = 1 - d % 2
        out_ref[0, d // 2, :, half * MOBA_BLOCK:(half + 1) * MOBA_BLOCK] = val


def _bias_tiles(rel_bias):
    n_heads = rel_bias.shape[1]
    return pl.pallas_call(
        _bias_kernel,
        grid=(n_heads,),
        in_specs=[pl.BlockSpec(memory_space=pltpu.SMEM)],
        out_specs=pl.BlockSpec((1, N_NEAR_PAIRS, MOBA_BLOCK, 2 * MOBA_BLOCK), lambda h: (h, 0, 0, 0)),
        out_shape=jax.ShapeDtypeStruct((n_heads, N_NEAR_PAIRS, MOBA_BLOCK, 2 * MOBA_BLOCK), F32),
        name="bias_tiles",
    )(rel_bias.astype(F32).reshape(-1))


def _inproj_kernel(x_ref, g_ref, w_ref, cos_ref, sin_ref, qg_ref, kg_ref, proj_ref, kmean_ref):
    x = x_ref[...]
    ms = jnp.mean(x * x, axis=-1, keepdims=True)
    h = (x * lax.rsqrt(ms + EPS) * g_ref[...]).astype(BF16)
    cos = cos_ref[...]
    sin = sin_ref[...]
    scale = HEAD_DIM ** -0.5
    rows = x.shape[0]

    def head_norm(y, gain):
        return y * lax.rsqrt(jnp.mean(y * y, axis=-1, keepdims=True) + EPS) * gain

    for s in range(N_SECTIONS):
        y = jnp.dot(h, w_ref[:, s * GROUP_WIDTH:(s + 1) * GROUP_WIDTH], preferred_element_type=F32)
        for hh in range(GROUP_WIDTH // HEAD_DIM):
            yh = y[:, hh * HEAD_DIM:(hh + 1) * HEAD_DIM]
            if s in (0, 1):
                yh = yh * cos + pltpu.roll(yh, HEAD_DIM // 2, axis=1) * sin
                if s == 1:
                    yh = yh * scale
            elif s == 4:
                yh = head_norm(yh, qg_ref[...]) * (scale * LOG2E)
            elif s == 5:
                yh = head_norm(yh, kg_ref[...])
                for blk in range(rows // MOBA_BLOCK):
                    kmean_ref[blk, :, hh * HEAD_DIM:(hh + 1) * HEAD_DIM] = jnp.mean(
                        yh[blk * MOBA_BLOCK:(blk + 1) * MOBA_BLOCK], axis=0, keepdims=True)
            col = s * GROUP_WIDTH + hh * HEAD_DIM
            proj_ref[:, col:col + HEAD_DIM] = yh.astype(BF16)


def _inproj(x2, norm_g, w_in, cos, sin, q_g, k_g, seq):
    m, d_model = x2.shape
    width = w_in.shape[1]
    tm = ROW_TILE
    seq_tiles = seq // tm
    return pl.pallas_call(
        _inproj_kernel,
        grid=(m // tm,),
        in_specs=[
            pl.BlockSpec((tm, d_model), lambda i: (i, 0)),
            pl.BlockSpec((1, d_model), lambda i: (0, 0)),
            pl.BlockSpec((d_model, width), lambda i: (0, 0)),
            pl.BlockSpec((tm, HEAD_DIM), lambda i: (i % seq_tiles, 0)),
            pl.BlockSpec((tm, HEAD_DIM), lambda i: (i % seq_tiles, 0)),
            pl.BlockSpec((1, HEAD_DIM), lambda i: (0, 0)),
            pl.BlockSpec((1, HEAD_DIM), lambda i: (0, 0)),
        ],
        out_specs=[
            pl.BlockSpec((tm, width), lambda i: (i, 0)),
            pl.BlockSpec((tm // MOBA_BLOCK, 1, GROUP_WIDTH), lambda i: (i, 0, 0)),
        ],
        out_shape=[
            jax.ShapeDtypeStruct((m, width), BF16),
            jax.ShapeDtypeStruct((m // MOBA_BLOCK, 1, GROUP_WIDTH), F32),
        ],
        compiler_params=pltpu.CompilerParams(
            dimension_semantics=("arbitrary",), vmem_limit_bytes=VMEM_LIMIT),
        name="in_projection",
    )(x2, norm_g.reshape(1, -1), w_in, cos, sin, q_g.reshape(1, -1), k_g.reshape(1, -1))


def _retention_kernel(q_ref, k_ref, v_ref, z_ref, dec_ref, dq_ref, dk_ref, cd_ref, g_ref,
                      o_ref, state_ref):
    @pl.when(pl.program_id(1) == 0)
    def _():
        state_ref[...] = jnp.zeros_like(state_ref)

    chunk = RET_CHUNK
    for c in range(q_ref.shape[0] // chunk):
        rows = slice(c * chunk, (c + 1) * chunk)
        for h in range(RET_HEADS):
            cols = slice(h * HEAD_DIM, (h + 1) * HEAD_DIM)
            q = q_ref[rows, cols]
            k = k_ref[rows, cols]
            v = v_ref[rows, cols]
            s = lax.dot_general(q, k, _NT, preferred_element_type=F32) * dec_ref[h]
            inner = jnp.dot(s.astype(BF16), v, preferred_element_type=F32)
            state = state_ref[h]
            cross = jnp.dot(q, state.astype(BF16), preferred_element_type=F32) * dq_ref[h]
            o = inner + cross
            v_dec = (v.astype(F32) * dk_ref[h]).astype(BF16)
            kv = lax.dot_general(k, v_dec, _TN, preferred_element_type=F32)
            state_ref[h] = state * cd_ref[h] + kv
            mu = jnp.mean(o, axis=-1, keepdims=True)
            oc = o - mu
            var = jnp.mean(oc * oc, axis=-1, keepdims=True)
            on = oc * lax.rsqrt(var + EPS) * g_ref[:, cols]
            z = z_ref[rows, cols].astype(F32)
            o_ref[rows, cols] = (z * jax.nn.sigmoid(z) * on).astype(BF16)


def _retention_tables():
    c = RET_CHUNK
    log_g = np.log1p(-np.power(2.0, -5.0 - np.arange(RET_HEADS, dtype=np.float64)))
    j = np.arange(c, dtype=np.float64)
    rel = j[:, None] - j[None, :]
    decay = np.where(rel >= 0, np.exp(log_g[:, None, None] * np.maximum(rel, 0.0)[None]), 0.0)
    dq = np.exp(log_g[:, None] * (j + 1.0)[None])
    dk = np.exp(log_g[:, None] * (c - 1 - j)[None])
    cd = np.exp(log_g * c)
    wide = lambda t: np.broadcast_to(t[:, :, None], (RET_HEADS, c, HEAD_DIM))
    return (jnp.asarray(decay, F32), jnp.asarray(wide(dq), F32), jnp.asarray(wide(dk), F32),
            jnp.asarray(np.broadcast_to(cd[:, None, None], (RET_HEADS, 1, HEAD_DIM)), F32))


def _retention(proj, gain, batch, seq):
    m = proj.shape[0]
    ts = RET_ROWS
    steps = seq // ts
    dec, dq, dk, cd = _retention_tables()
    section = lambda sec: pl.BlockSpec((ts, GROUP_WIDTH), lambda b, i: (b * steps + i, sec))
    full = lambda a: pl.BlockSpec(a.shape, lambda b, i: (0,) * a.ndim)
    return pl.pallas_call(
        _retention_kernel,
        grid=(batch, steps),
        in_specs=[section(0), section(1), section(2), section(3),
                  full(dec), full(dq), full(dk), full(cd),
                  pl.BlockSpec((1, GROUP_WIDTH), lambda b, i: (0, 0))],
        out_specs=pl.BlockSpec((ts, GROUP_WIDTH), lambda b, i: (b * steps + i, 0)),
        out_shape=jax.ShapeDtypeStruct((m, GROUP_WIDTH), BF16),
        scratch_shapes=[pltpu.VMEM((RET_HEADS, HEAD_DIM, HEAD_DIM), F32)],
        compiler_params=pltpu.CompilerParams(
            dimension_semantics=("arbitrary", "arbitrary"), vmem_limit_bytes=VMEM_LIMIT),
        name="retention",
    )(proj, proj, proj, proj, dec, dq, dk, cd, gain.reshape(1, -1))


def _moba_kernel(rb_ref, q_ref, k_ref, v_ref, z_ref, km_ref, oh_ref, bias_ref, o_ref,
                 qaug_scr, m_scr, acc_scr, s_scr):
    blk = MOBA_BLOCK
    qi = pl.program_id(1)
    n_heads = MOBA_HEADS
    blk_id = lax.broadcasted_iota(jnp.int32, (SEL_LANES, blk), 0)
    blk_f = blk_id.astype(F32)
    past = blk_id < qi
    far = (qi - blk_id) >= N_NEAR_BLOCKS

    for h in range(n_heads):
        cols = slice(h * HEAD_DIM, (h + 1) * HEAD_DIM)
        q = q_ref[:, cols]
        km = km_ref[0, :, cols]
        km_hi = km.astype(BF16)
        km_lo = (km - km_hi.astype(F32)).astype(BF16)
        gate = lax.dot_general(jnp.concatenate([km_hi, km_lo], axis=1),
                               jnp.concatenate([q, q], axis=1), _NT,
                               preferred_element_type=F32)
        g = jnp.where(past, gate, MASKED_GATE)
        sel = jnp.zeros(gate.shape, F32)
        for _ in range(MOBA_TOPK):
            top = jnp.max(g, axis=0, keepdims=True)
            first = jnp.min(jnp.where(g == top, blk_f, 1e9), axis=0, keepdims=True)
            pick = blk_f == first
            sel = jnp.where(pick, 1.0, sel)
            g = jnp.where(pick, MASKED_GATE, g)
        chosen = jnp.where(past, sel, 0.0) > 0.0
        far_c = jnp.full(gate.shape, rb_ref[(N_BUCKETS - 1) * n_heads + h] * LOG2E, F32)
        far_hi = far_c.astype(BF16).astype(F32)
        mask_hi = jnp.where(chosen, jnp.where(far, far_hi, 0.0), NEG_INF)
        mask_hi = jnp.where(blk_id == qi, 0.0, mask_hi)
        mask_lo = jnp.where(chosen, jnp.where(far, far_c - far_hi, 0.0), 0.0)
        mask_row = jnp.concatenate([mask_hi, mask_lo], axis=0).T
        qaug_scr[h] = jnp.concatenate([q, mask_row.astype(BF16)], axis=1)

    m_scr[...] = jnp.full(m_scr.shape, NEG_INF, F32)
    acc_scr[...] = jnp.zeros(acc_scr.shape, F32)
    ones = jnp.ones((2 * blk, HEAD_DIM), BF16)
    n_blocks = k_ref.shape[0] // blk

    def pair_rows(pair):
        j_new = qi - 2 * pair
        j_old = j_new - 1
        start = lambda j: pl.multiple_of(jnp.maximum(j, 0) * blk, blk)
        ident = lambda j: pl.multiple_of(jnp.where(j < 0, n_blocks, j) * blk, blk)
        return start(j_old), start(j_new), ident(j_old), ident(j_new)

    def scores(pair):
        old, new, old_id, new_id = pair_rows(pair)
        slot = pair & 1
        onehot = jnp.concatenate([oh_ref[pl.ds(old_id, blk), :], oh_ref[pl.ds(new_id, blk), :]],
                                 axis=0)
        for h in range(n_heads):
            cols = slice(h * HEAD_DIM, (h + 1) * HEAD_DIM)
            k_pair = jnp.concatenate([k_ref[pl.ds(old, blk), cols], k_ref[pl.ds(new, blk), cols]], axis=0)
            k_aug = jnp.concatenate([k_pair, onehot], axis=1)
            s_scr[slot, h] = lax.dot_general(qaug_scr[h], k_aug, _NT, preferred_element_type=F32)

    def accumulate(pair, with_bias):
        old, new, _, _ = pair_rows(pair)
        slot = pair & 1
        for h in range(n_heads):
            cols = slice(h * HEAD_DIM, (h + 1) * HEAD_DIM)
            v_pair = jnp.concatenate([v_ref[pl.ds(old, blk), cols], v_ref[pl.ds(new, blk), cols]], axis=0)
            v_aug = jnp.concatenate([v_pair, ones], axis=1)
            s = s_scr[slot, h]
            if with_bias:
                s = s + bias_ref[h, pair]
            m_prev = m_scr[h]
            m_next = jnp.maximum(m_prev, jnp.max(s, axis=1, keepdims=True))
            alpha = jnp.exp2(m_prev - m_next)
            p = jnp.exp2(s - jnp.concatenate([m_next] * (2 * blk // HEAD_DIM), axis=1))
            acc_scr[h] = (jnp.concatenate([alpha, alpha], axis=1) * acc_scr[h]
                          + jnp.dot(p.astype(BF16), v_aug, preferred_element_type=F32))
            m_scr[h] = m_next

    def near_body(pair, carry):
        accumulate(pair, True)
        scores(pair + 1)
        return carry

    def far_body(pair, carry):
        accumulate(pair, False)
        scores(pair + 1)
        return carry

    n_pairs = lax.shift_right_logical(qi + 2, 1)
    n_near = jnp.minimum(n_pairs, N_NEAR_PAIRS)
    scores(0)
    lax.fori_loop(0, n_near, near_body, 0)
    lax.fori_loop(n_near, n_pairs, far_body, 0)
    for h in range(n_heads):
        cols = slice(h * HEAD_DIM, (h + 1) * HEAD_DIM)
        z = z_ref[:, cols].astype(F32)
        out = acc_scr[h, :, :HEAD_DIM] / acc_scr[h, :, HEAD_DIM:]
        o_ref[:, cols] = (z * jax.nn.sigmoid(z) * out).astype(BF16)


def _moba(proj, kmean, bias_tiles, rel_bias, batch, seq):
    m = proj.shape[0]
    blk = MOBA_BLOCK
    nq = seq // blk
    row_blk = lax.broadcasted_iota(jnp.int32, (seq + blk, HEAD_DIM), 0) // blk
    lane = lax.broadcasted_iota(jnp.int32, (seq + blk, HEAD_DIM), 1)
    onehot = ((lane & (SEL_LANES - 1)) == row_blk).astype(BF16)
    tile = lambda sec: pl.BlockSpec((blk, GROUP_WIDTH), lambda b, i: (b * nq + i, sec))
    whole = lambda sec: pl.BlockSpec((seq, GROUP_WIDTH), lambda b, i: (b, sec))
    return pl.pallas_call(
        _moba_kernel,
        grid=(batch, nq),
        in_specs=[pl.BlockSpec(memory_space=pltpu.SMEM),
                  tile(4), whole(5), whole(6), tile(7),
                  pl.BlockSpec((1, SEL_LANES, GROUP_WIDTH), lambda b, i: (b, 0, 0)),
                  pl.BlockSpec(onehot.shape, lambda b, i: (0, 0), pipeline_mode=pl.Buffered(1)),
                  pl.BlockSpec(bias_tiles.shape, lambda b, i: (0, 0, 0, 0),
                               pipeline_mode=pl.Buffered(1))],
        out_specs=pl.BlockSpec((blk, GROUP_WIDTH), lambda b, i: (b * nq + i, 0)),
        out_shape=jax.ShapeDtypeStruct((m, GROUP_WIDTH), BF16),
        scratch_shapes=[pltpu.VMEM((MOBA_HEADS, blk, 2 * HEAD_DIM), BF16),
                        pltpu.VMEM((MOBA_HEADS, blk, HEAD_DIM), F32),
                        pltpu.VMEM((MOBA_HEADS, blk, 2 * HEAD_DIM), F32),
                        pltpu.VMEM((2, MOBA_HEADS, blk, 2 * blk), F32)],
        compiler_params=pltpu.CompilerParams(
            dimension_semantics=("arbitrary", "arbitrary"), vmem_limit_bytes=VMEM_LIMIT),
        name="moba",
    )(rel_bias.astype(F32).reshape(-1), proj, proj, proj, proj, kmean, onehot, bias_tiles)


def _outproj_kernel(x_ref, r_ref, a_ref, wo_ref, p_ref, wp_ref, pg_ref, wg_ref, o_ref):
    def unit_rms(t):
        return t * lax.rsqrt(jnp.mean(t * t, axis=-1, keepdims=True) + EPS)

    y = (jnp.dot(r_ref[...], wo_ref[:GROUP_WIDTH, :], preferred_element_type=F32)
         + jnp.dot(a_ref[...], wo_ref[GROUP_WIDTH:, :], preferred_element_type=F32))
    x1 = x_ref[...] + y
    gate = jax.nn.sigmoid(jnp.dot(unit_rms(x1).astype(BF16), wg_ref[...],
                                  preferred_element_type=F32))
    e = unit_rms(jnp.dot(p_ref[...].astype(BF16), wp_ref[...], preferred_element_type=F32))
    o_ref[...] = x1 + gate * (e * pg_ref[...])


def _outproj(x2, r_out, m_out, w_out, p2, w_ple, ple_g, w_gate):
    m, d_model = x2.shape
    tm = ROW_TILE
    rows = lambda a: pl.BlockSpec((tm, a.shape[1]), lambda i: (i, 0))
    full = lambda a: pl.BlockSpec(a.shape, lambda i: (0, 0))
    ple_g = ple_g.reshape(1, -1)
    return pl.pallas_call(
        _outproj_kernel,
        grid=(m // tm,),
        in_specs=[rows(x2), rows(r_out), rows(m_out), full(w_out), rows(p2), full(w_ple),
                  full(ple_g), full(w_gate)],
        out_specs=pl.BlockSpec((tm, d_model), lambda i: (i, 0)),
        out_shape=jax.ShapeDtypeStruct((m, d_model), F32),
        compiler_params=pltpu.CompilerParams(
            dimension_semantics=("arbitrary",), vmem_limit_bytes=VMEM_LIMIT),
        name="out_projection",
    )(x2, r_out, m_out, w_out, p2, w_ple, ple_g, w_gate)


def _rotary_tables(seq):
    half = HEAD_DIM // 2
    inv = ROPE_THETA ** (-jnp.arange(half, dtype=F32) / half)
    ang = jnp.arange(seq, dtype=F32)[:, None] * inv[None, :]
    cos, sin = jnp.cos(ang), jnp.sin(ang)
    return jnp.concatenate([cos, cos], axis=1), jnp.concatenate([-sin, sin], axis=1)


def kernel(x, p, norm_g, w_in, ret_norm_g, q_norm_g, k_norm_g, rel_bias, w_out, w_ple, ple_norm_g, w_ple_gate):
    batch, seq, d_model = x.shape
    depth = w_in.shape[0]
    m = batch * seq
    assert w_in.shape[2] == N_SECTIONS * GROUP_WIDTH and w_out.shape[1] == 2 * GROUP_WIDTH
    assert rel_bias.shape == (N_BUCKETS, MOBA_HEADS)
    assert seq % ROW_TILE == 0 and seq % RET_ROWS == 0 and ROW_TILE % MOBA_BLOCK == 0
    assert RET_ROWS % RET_CHUNK == 0 and seq // MOBA_BLOCK < SEL_LANES

    cos, sin = _rotary_tables(seq)
    bias_tiles = _bias_tiles(rel_bias)
    x2 = x.reshape(m, d_model)
    nb = seq // MOBA_BLOCK
    for i in range(depth):
        proj, kmean = _inproj(x2, norm_g[i], w_in[i].astype(BF16), cos, sin,
                              q_norm_g[i], k_norm_g[i], seq)
        kmean = jnp.pad(kmean.reshape(batch, nb, GROUP_WIDTH), ((0, 0), (0, SEL_LANES - nb), (0, 0)))
        r_out = _retention(proj, ret_norm_g[i], batch, seq)
        m_out = _moba(proj, kmean, bias_tiles, rel_bias, batch, seq)
        x2 = _outproj(x2, r_out, m_out, w_out[i].astype(BF16), p[i].reshape(m, -1),
                      w_ple[i].astype(BF16), ple_norm_g[i], w_ple_gate[i].astype(BF16))
    return x2.reshape(batch, seq, d_model)
```

```python
import functools
import math

import numpy as np
import jax
import jax.numpy as jnp
from jax import lax
from jax.experimental import pallas as pl
from jax.experimental.pallas import tpu as pltpu

F32 = jnp.float32
BF16 = jnp.bfloat16

HEAD_DIM = 128
RET_HEADS = 4
MOBA_HEADS = 4
GROUP_WIDTH = 512
N_SECTIONS = 8
RET_SECTIONS = 4
RET_CHUNK = 128
MOBA_BLOCK = 256
MOBA_TOPK = 3
N_BUCKETS = 32
MAX_DISTANCE = 2048
ROPE_THETA = 10000.0
EPS = 1e-6
NEG_INF = -1e30
MASKED_GATE = -3e38
LOG2E = math.log2(math.e)
SEL_LANES = HEAD_DIM // 2

ROW_TILE = 512
CAST_ROWS = 256
VMEM_LIMIT = 56 * 1024 * 1024

_NT = (((1,), (1,)), ((), ()))
_TN = (((0,), (0,)), ((), ()))


def _bucket_thresholds():
    max_exact = N_BUCKETS // 2
    thr = []
    for k in range(N_BUCKETS - max_exact):
        n = max_exact
        while min(max_exact + int(math.log(n / max_exact) / math.log(MAX_DISTANCE / max_exact)
                                  * (N_BUCKETS - max_exact)), N_BUCKETS - 1) < max_exact + k:
            n += 1
        thr.append(n)
    return thr


_BUCKET_THR = _bucket_thresholds()
N_NEAR = next(d for d in range(1, 1 << 20) if (d - 1) * MOBA_BLOCK + 1 >= _BUCKET_THR[-1])
N_NEAR_PAIRS = (N_NEAR + 1) // 2
N_NEAR_BLOCKS = 2 * N_NEAR_PAIRS


def _bias_kernel(rb_ref, out_ref):
    h = pl.program_id(0)
    n_heads = pl.num_programs(0)
    r = lax.broadcasted_iota(jnp.int32, (MOBA_BLOCK, MOBA_BLOCK), 0)
    c = lax.broadcasted_iota(jnp.int32, (MOBA_BLOCK, MOBA_BLOCK), 1)
    max_exact = N_BUCKETS // 2

    def rb(b):
        return rb_ref[b * n_heads + h] * LOG2E

    for d in range(N_NEAR_BLOCKS):
        n = d * MOBA_BLOCK + r - c
        nn = jnp.maximum(n, 0)
        val = jnp.full((MOBA_BLOCK, MOBA_BLOCK), rb(0), F32)
        for b in range(1, max_exact):
            val = jnp.where(nn >= b, rb(b), val)
        for k, thr in enumerate(_BUCKET_THR):
            val = jnp.where(nn >= thr, rb(max_exact + k), val)
        if d == 0:
            val = jnp.where(n >= 0, val, NEG_INF)
        half = 1 - d % 2
        out_ref[0, d // 2, :, half * MOBA_BLOCK:(half + 1) * MOBA_BLOCK] = val


def _bias_tiles(rel_bias):
    n_heads = rel_bias.shape[1]
    return pl.pallas_call(
        _bias_kernel,
        grid=(n_heads,),
        in_specs=[pl.BlockSpec(memory_space=pltpu.SMEM)],
        out_specs=pl.BlockSpec((1, N_NEAR_PAIRS, MOBA_BLOCK, 2 * MOBA_BLOCK), lambda h: (h, 0, 0, 0)),
        out_shape=jax.ShapeDtypeStruct((n_heads, N_NEAR_PAIRS, MOBA_BLOCK, 2 * MOBA_BLOCK), F32),
        name="bias_tiles",
    )(rel_bias.astype(F32).reshape(-1))


def _cast_weights_once(pairs):
    @pl.when(pl.program_id(0) == 0)
    def _():
        for w_ref, w_scr in pairs:
            for r in range(0, w_ref.shape[0], CAST_ROWS):
                w_scr[r:r + CAST_ROWS, :] = w_ref[r:r + CAST_ROWS, :].astype(BF16)


def _inproj_kernel(x_ref, g_ref, w_ref, cos_ref, sin_ref, qg_ref, kg_ref,
                   dec_ref, dq_ref, dk_ref, cd_ref, rg_ref,
                   proj_ref, kmean_ref, ret_ref, w_scr, rin_scr, state_ref, *, seq_tiles):
    _cast_weights_once([(w_ref, w_scr)])
    x = x_ref[...]
    ms = jnp.mean(x * x, axis=-1, keepdims=True)
    h = (x * lax.rsqrt(ms + EPS) * g_ref[...]).astype(BF16)
    cos = cos_ref[...]
    sin = sin_ref[...]
    scale = HEAD_DIM ** -0.5
    rows = x.shape[0]
    heads = GROUP_WIDTH // HEAD_DIM

    def head_norm(y, gain):
        return y * lax.rsqrt(jnp.mean(y * y, axis=-1, keepdims=True) + EPS) * gain

    def section(s):
        return jnp.dot(h, w_scr[:, s * GROUP_WIDTH:(s + 1) * GROUP_WIDTH], preferred_element_type=F32)

    for s in range(RET_SECTIONS):
        y = section(s)
        for hh in range(heads):
            yh = y[:, hh * HEAD_DIM:(hh + 1) * HEAD_DIM]
            if s in (0, 1):
                yh = yh * cos + pltpu.roll(yh, HEAD_DIM // 2, axis=1) * sin
                if s == 1:
                    yh = yh * scale
            col = s * GROUP_WIDTH + hh * HEAD_DIM
            rin_scr[:, col:col + HEAD_DIM] = yh.astype(BF16)

    @pl.when(pl.program_id(0) % seq_tiles == 0)
    def _():
        state_ref[...] = jnp.zeros_like(state_ref)

    def moba_section(s):
        y = section(s)
        for hh in range(heads):
            yh = y[:, hh * HEAD_DIM:(hh + 1) * HEAD_DIM]
            if s == 4:
                yh = head_norm(yh, qg_ref[...]) * (scale * LOG2E)
            elif s == 5:
                yh = head_norm(yh, kg_ref[...])
                for blk in range(rows // MOBA_BLOCK):
                    kmean_ref[blk, :, hh * HEAD_DIM:(hh + 1) * HEAD_DIM] = jnp.mean(
                        yh[blk * MOBA_BLOCK:(blk + 1) * MOBA_BLOCK], axis=0, keepdims=True)
            col = (s - RET_SECTIONS) * GROUP_WIDTH + hh * HEAD_DIM
            proj_ref[:, col:col + HEAD_DIM] = yh.astype(BF16)

    chunk = RET_CHUNK
    units = [(c, hh) for c in range(rows // chunk) for hh in range(RET_HEADS)]

    def operand(c, hh, sec):
        return rin_scr[c * chunk:(c + 1) * chunk,
                       sec * GROUP_WIDTH + hh * HEAD_DIM:sec * GROUP_WIDTH + (hh + 1) * HEAD_DIM]

    masked, kv = {}, {}
    for c, hh in units:
        q, k, v = operand(c, hh, 0), operand(c, hh, 1), operand(c, hh, 2)
        sc = lax.dot_general(q, k, _NT, preferred_element_type=F32) * dec_ref[hh]
        masked[c, hh] = sc.astype(BF16)
        v_dec = (v.astype(F32) * dk_ref[hh]).astype(BF16)
        kv[c, hh] = lax.dot_general(k, v_dec, _TN, preferred_element_type=F32)
    before = {}
    for hh in range(RET_HEADS):
        state = state_ref[hh]
        for c in range(rows // chunk):
            before[c, hh] = state.astype(BF16)
            state = state * cd_ref[hh] + kv[c, hh]
        state_ref[hh] = state
    for c, hh in units:
        q, v = operand(c, hh, 0), operand(c, hh, 2)
        q_dec = (q.astype(F32) * dq_ref[hh]).astype(BF16)
        o = jnp.dot(jnp.concatenate([masked[c, hh], q_dec], axis=1),
                    jnp.concatenate([v, before[c, hh]], axis=0), preferred_element_type=F32)
        mu = jnp.mean(o, axis=-1, keepdims=True)
        oc = o - mu
        var = jnp.mean(oc * oc, axis=-1, keepdims=True)
        on = oc * lax.rsqrt(var + EPS) * rg_ref[:, hh * HEAD_DIM:(hh + 1) * HEAD_DIM]
        z = operand(c, hh, 3).astype(F32)
        ret_ref[c * chunk:(c + 1) * chunk, hh * HEAD_DIM:(hh + 1) * HEAD_DIM] = (
            z * jax.nn.sigmoid(z) * on).astype(BF16)

    for s in range(RET_SECTIONS, N_SECTIONS):
        moba_section(s)


def _retention_tables():
    c = RET_CHUNK
    log_g = np.log1p(-np.power(2.0, -5.0 - np.arange(RET_HEADS, dtype=np.float64)))
    j = np.arange(c, dtype=np.float64)
    rel = j[:, None] - j[None, :]
    decay = np.where(rel >= 0, np.exp(log_g[:, None, None] * np.maximum(rel, 0.0)[None]), 0.0)
    dq = np.exp(log_g[:, None] * (j + 1.0)[None])
    dk = np.exp(log_g[:, None] * (c - 1 - j)[None])
    cd = np.exp(log_g * c)
    wide = lambda t: np.broadcast_to(t[:, :, None], (RET_HEADS, c, HEAD_DIM))
    return (jnp.asarray(decay, F32), jnp.asarray(wide(dq), F32), jnp.asarray(wide(dk), F32),
            jnp.asarray(np.broadcast_to(cd[:, None, None], (RET_HEADS, 1, HEAD_DIM)), F32))


def _inproj(x2, norm_g, w_in, cos, sin, q_g, k_g, ret_g, seq):
    m, d_model = x2.shape
    width = w_in.shape[1]
    moba_width = width - RET_SECTIONS * GROUP_WIDTH
    tm = ROW_TILE
    seq_tiles = seq // tm
    tables = _retention_tables()
    full = lambda a: pl.BlockSpec(a.shape, lambda i: (0,) * a.ndim)
    row1 = lambda n: pl.BlockSpec((1, n), lambda i: (0, 0))
    return pl.pallas_call(
        functools.partial(_inproj_kernel, seq_tiles=seq_tiles),
        grid=(m // tm,),
        in_specs=[
            pl.BlockSpec((tm, d_model), lambda i: (i, 0)),
            row1(d_model),
            pl.BlockSpec((d_model, width), lambda i: (0, 0), pipeline_mode=pl.Buffered(1)),
            pl.BlockSpec((tm, HEAD_DIM), lambda i: (i % seq_tiles, 0)),
            pl.BlockSpec((tm, HEAD_DIM), lambda i: (i % seq_tiles, 0)),
            row1(HEAD_DIM), row1(HEAD_DIM),
            *[full(t) for t in tables],
            row1(GROUP_WIDTH),
        ],
        out_specs=[
            pl.BlockSpec((tm, moba_width), lambda i: (i, 0)),
            pl.BlockSpec((tm // MOBA_BLOCK, 1, GROUP_WIDTH), lambda i: (i, 0, 0)),
            pl.BlockSpec((tm, GROUP_WIDTH), lambda i: (i, 0)),
        ],
        out_shape=[
            jax.ShapeDtypeStruct((m, moba_width), BF16),
            jax.ShapeDtypeStruct((m // MOBA_BLOCK, 1, GROUP_WIDTH), F32),
            jax.ShapeDtypeStruct((m, GROUP_WIDTH), BF16),
        ],
        scratch_shapes=[pltpu.VMEM((d_model, width), BF16),
                        pltpu.VMEM((tm, RET_SECTIONS * GROUP_WIDTH), BF16),
                        pltpu.VMEM((RET_HEADS, HEAD_DIM, HEAD_DIM), F32)],
        compiler_params=pltpu.CompilerParams(
            dimension_semantics=("arbitrary",), vmem_limit_bytes=VMEM_LIMIT),
        name="in_projection",
    )(x2, norm_g.reshape(1, -1), w_in, cos, sin, q_g.reshape(1, -1), k_g.reshape(1, -1),
      *tables, ret_g.reshape(1, -1))


def _moba_kernel(rb_ref, q_ref, k_ref, v_ref, z_ref, km_ref, oh_ref, bias_ref, o_ref,
                 qaug_scr, m_scr, acc_scr, s_scr):
    blk = MOBA_BLOCK
    qi = pl.program_id(1)
    n_heads = MOBA_HEADS
    blk_id = lax.broadcasted_iota(jnp.int32, (SEL_LANES, blk), 0)
    blk_f = blk_id.astype(F32)
    past = blk_id < qi
    far = (qi - blk_id) >= N_NEAR_BLOCKS

    for h in range(n_heads):
        cols = slice(h * HEAD_DIM, (h + 1) * HEAD_DIM)
        q = q_ref[:, cols]
        km = km_ref[0, :, cols]
        km_hi = km.astype(BF16)
        km_lo = (km - km_hi.astype(F32)).astype(BF16)
        gate = lax.dot_general(jnp.concatenate([km_hi, km_lo], axis=1),
                               jnp.concatenate([q, q], axis=1), _NT,
                               preferred_element_type=F32)
        g = jnp.where(past, gate, MASKED_GATE)
        sel = jnp.zeros(gate.shape, F32)
        for _ in range(MOBA_TOPK):
            top = jnp.max(g, axis=0, keepdims=True)
            first = jnp.min(jnp.where(g == top, blk_f, 1e9), axis=0, keepdims=True)
            pick = blk_f == first
            sel = jnp.where(pick, 1.0, sel)
            g = jnp.where(pick, MASKED_GATE, g)
        chosen = jnp.where(past, sel, 0.0) > 0.0
        far_c = jnp.full(gate.shape, rb_ref[(N_BUCKETS - 1) * n_heads + h] * LOG2E, F32)
        far_hi = far_c.astype(BF16).astype(F32)
        mask_hi = jnp.where(chosen, jnp.where(far, far_hi, 0.0), NEG_INF)
        mask_hi = jnp.where(blk_id == qi, 0.0, mask_hi)
        mask_lo = jnp.where(chosen, jnp.where(far, far_c - far_hi, 0.0), 0.0)
        mask_row = jnp.concatenate([mask_hi, mask_lo], axis=0).T
        qaug_scr[h] = jnp.concatenate([q, mask_row.astype(BF16)], axis=1)

    m_scr[...] = jnp.full(m_scr.shape, NEG_INF, F32)
    acc_scr[...] = jnp.zeros(acc_scr.shape, F32)
    ones = jnp.ones((2 * blk, HEAD_DIM), BF16)
    n_blocks = k_ref.shape[0] // blk

    def pair_rows(pair):
        j_new = qi - 2 * pair
        j_old = j_new - 1
        start = lambda j: pl.multiple_of(jnp.maximum(j, 0) * blk, blk)
        ident = lambda j: pl.multiple_of(jnp.where(j < 0, n_blocks, j) * blk, blk)
        return start(j_old), start(j_new), ident(j_old), ident(j_new)

    def scores(pair):
        old, new, old_id, new_id = pair_rows(pair)
        slot = pair & 1
        onehot = jnp.concatenate([oh_ref[pl.ds(old_id, blk), :], oh_ref[pl.ds(new_id, blk), :]],
                                 axis=0)
        for h in range(n_heads):
            cols = slice(h * HEAD_DIM, (h + 1) * HEAD_DIM)
            k_pair = jnp.concatenate([k_ref[pl.ds(old, blk), cols], k_ref[pl.ds(new, blk), cols]], axis=0)
            k_aug = jnp.concatenate([k_pair, onehot], axis=1)
            s_scr[slot, h] = lax.dot_general(qaug_scr[h], k_aug, _NT, preferred_element_type=F32)

    def accumulate(pair, with_bias):
        old, new, _, _ = pair_rows(pair)
        slot = pair & 1
        for h in range(n_heads):
            cols = slice(h * HEAD_DIM, (h + 1) * HEAD_DIM)
            v_pair = jnp.concatenate([v_ref[pl.ds(old, blk), cols], v_ref[pl.ds(new, blk), cols]], axis=0)
            v_aug = jnp.concatenate([v_pair, ones], axis=1)
            s = s_scr[slot, h]
            if with_bias:
                s = s + bias_ref[h, pair]
            m_prev = m_scr[h]
            m_next = jnp.maximum(m_prev, jnp.max(s, axis=1, keepdims=True))
            alpha = jnp.exp2(m_prev - m_next)
            p = jnp.exp2(s - jnp.concatenate([m_next] * (2 * blk // HEAD_DIM), axis=1))
            acc_scr[h] = (jnp.concatenate([alpha, alpha], axis=1) * acc_scr[h]
                          + jnp.dot(p.astype(BF16), v_aug, preferred_element_type=F32))
            m_scr[h] = m_next

    def near_body(pair, carry):
        accumulate(pair, True)
        scores(pair + 1)
        return carry

    def far_body(pair, carry):
        accumulate(pair, False)
        scores(pair + 1)
        return carry

    n_pairs = lax.shift_right_logical(qi + 2, 1)
    n_near = jnp.minimum(n_pairs, N_NEAR_PAIRS)
    scores(0)
    lax.fori_loop(0, n_near, near_body, 0)
    lax.fori_loop(n_near, n_pairs, far_body, 0)
    for h in range(n_heads):
        cols = slice(h * HEAD_DIM, (h + 1) * HEAD_DIM)
        z = z_ref[:, cols].astype(F32)
        out = acc_scr[h, :, :HEAD_DIM] / acc_scr[h, :, HEAD_DIM:]
        o_ref[:, cols] = (z * jax.nn.sigmoid(z) * out).astype(BF16)


def _moba(proj, kmean, bias_tiles, rel_bias, batch, seq):
    m = proj.shape[0]
    blk = MOBA_BLOCK
    nq = seq // blk
    row_blk = lax.broadcasted_iota(jnp.int32, (seq + blk, HEAD_DIM), 0) // blk
    lane = lax.broadcasted_iota(jnp.int32, (seq + blk, HEAD_DIM), 1)
    onehot = ((lane & (SEL_LANES - 1)) == row_blk).astype(BF16)
    tile = lambda sec: pl.BlockSpec((blk, GROUP_WIDTH), lambda b, i: (b * nq + i, sec))
    whole = lambda sec: pl.BlockSpec((seq, GROUP_WIDTH), lambda b, i: (b, sec))
    return pl.pallas_call(
        _moba_kernel,
        grid=(batch, nq),
        in_specs=[pl.BlockSpec(memory_space=pltpu.SMEM),
                  tile(0), whole(1), whole(2), tile(3),
                  pl.BlockSpec((1, SEL_LANES, GROUP_WIDTH), lambda b, i: (b, 0, 0)),
                  pl.BlockSpec(onehot.shape, lambda b, i: (0, 0), pipeline_mode=pl.Buffered(1)),
                  pl.BlockSpec(bias_tiles.shape, lambda b, i: (0, 0, 0, 0),
                               pipeline_mode=pl.Buffered(1))],
        out_specs=pl.BlockSpec((blk, GROUP_WIDTH), lambda b, i: (b * nq + i, 0)),
        out_shape=jax.ShapeDtypeStruct((m, GROUP_WIDTH), BF16),
        scratch_shapes=[pltpu.VMEM((MOBA_HEADS, blk, 2 * HEAD_DIM), BF16),
                        pltpu.VMEM((MOBA_HEADS, blk, HEAD_DIM), F32),
                        pltpu.VMEM((MOBA_HEADS, blk, 2 * HEAD_DIM), F32),
                        pltpu.VMEM((2, MOBA_HEADS, blk, 2 * blk), F32)],
        compiler_params=pltpu.CompilerParams(
            dimension_semantics=("arbitrary", "arbitrary"), vmem_limit_bytes=VMEM_LIMIT),
        name="moba",
    )(rel_bias.astype(F32).reshape(-1), proj, proj, proj, proj, kmean, onehot, bias_tiles)


def _outproj_kernel(x_ref, r_ref, a_ref, wo_ref, p_ref, wp_ref, pg_ref, wg_ref, o_ref,
                    wo_scr, wp_scr, wg_scr):
    _cast_weights_once([(wo_ref, wo_scr), (wp_ref, wp_scr), (wg_ref, wg_scr)])

    def unit_rms(t):
        return t * lax.rsqrt(jnp.mean(t * t, axis=-1, keepdims=True) + EPS)

    y = (jnp.dot(r_ref[...], wo_scr[:GROUP_WIDTH, :], preferred_element_type=F32)
         + jnp.dot(a_ref[...], wo_scr[GROUP_WIDTH:, :], preferred_element_type=F32))
    x1 = x_ref[...] + y
    gate = jax.nn.sigmoid(jnp.dot(unit_rms(x1).astype(BF16), wg_scr[...],
                                  preferred_element_type=F32))
    e = unit_rms(jnp.dot(p_ref[...].astype(BF16), wp_scr[...], preferred_element_type=F32))
    o_ref[...] = x1 + gate * (e * pg_ref[...])


def _outproj(x2, r_out, m_out, w_out, p2, w_ple, ple_g, w_gate):
    m, d_model = x2.shape
    tm = ROW_TILE
    rows = lambda a: pl.BlockSpec((tm, a.shape[1]), lambda i: (i, 0))
    full = lambda a: pl.BlockSpec(a.shape, lambda i: (0, 0), pipeline_mode=pl.Buffered(1))
    ple_g = ple_g.reshape(1, -1)
    return pl.pallas_call(
        _outproj_kernel,
        grid=(m // tm,),
        in_specs=[rows(x2), rows(r_out), rows(m_out), full(w_out), rows(p2), full(w_ple),
                  full(ple_g), full(w_gate)],
        out_specs=pl.BlockSpec((tm, d_model), lambda i: (i, 0)),
        out_shape=jax.ShapeDtypeStruct((m, d_model), F32),
        scratch_shapes=[pltpu.VMEM(w.shape, BF16) for w in (w_out, w_ple, w_gate)],
        compiler_params=pltpu.CompilerParams(
            dimension_semantics=("arbitrary",), vmem_limit_bytes=VMEM_LIMIT),
        name="out_projection",
    )(x2, r_out, m_out, w_out, p2, w_ple, ple_g, w_gate)


def _rotary_tables(seq):
    half = HEAD_DIM // 2
    inv = ROPE_THETA ** (-jnp.arange(half, dtype=F32) / half)
    ang = jnp.arange(seq, dtype=F32)[:, None] * inv[None, :]
    cos, sin = jnp.cos(ang), jnp.sin(ang)
    return jnp.concatenate([cos, cos], axis=1), jnp.concatenate([-sin, sin], axis=1)


def kernel(x, p, norm_g, w_in, ret_norm_g, q_norm_g, k_norm_g, rel_bias, w_out, w_ple, ple_norm_g, w_ple_gate):
    batch, seq, d_model = x.shape
    depth = w_in.shape[0]
    m = batch * seq
    assert w_in.shape[2] == N_SECTIONS * GROUP_WIDTH and w_out.shape[1] == 2 * GROUP_WIDTH
    assert rel_bias.shape == (N_BUCKETS, MOBA_HEADS)
    assert seq % ROW_TILE == 0 and ROW_TILE % MOBA_BLOCK == 0
    assert ROW_TILE % RET_CHUNK == 0 and seq // MOBA_BLOCK < SEL_LANES

    cos, sin = _rotary_tables(seq)
    bias_tiles = _bias_tiles(rel_bias)
    x2 = x.reshape(m, d_model)
    nb = seq // MOBA_BLOCK
    for i in range(depth):
        proj, kmean, r_out = _inproj(x2, norm_g[i], w_in[i], cos, sin,
                                     q_norm_g[i], k_norm_g[i], ret_norm_g[i], seq)
        kmean = jnp.pad(kmean.reshape(batch, nb, GROUP_WIDTH), ((0, 0), (0, SEL_LANES - nb), (0, 0)))
        m_out = _moba(proj, kmean, bias_tiles, rel_bias, batch, seq)
        x2 = _outproj(x2, r_out, m_out, w_out[i], p[i].reshape(m, -1),
                      w_ple[i], ple_norm_g[i], w_ple_gate[i])
    return x2.reshape(batch, seq, d_model)
```

```python
import functools
import math

import numpy as np
import jax
import jax.numpy as jnp
from jax import lax
from jax.experimental import pallas as pl
from jax.experimental.pallas import tpu as pltpu

F32 = jnp.float32
BF16 = jnp.bfloat16

HEAD_DIM = 128
RET_HEADS = 4
MOBA_HEADS = 4
GROUP_WIDTH = 512
N_SECTIONS = 8
RET_SECTIONS = 4
RET_CHUNK = 128
MOBA_BLOCK = 256
MOBA_TOPK = 3
N_BUCKETS = 32
MAX_DISTANCE = 2048
ROPE_THETA = 10000.0
EPS = 1e-6
NEG_INF = -1e30
MASKED_GATE = -3e38
LOG2E = math.log2(math.e)
SEL_LANES = HEAD_DIM // 2

ROW_TILE = 512
CAST_ROWS = 256
VMEM_LIMIT = 56 * 1024 * 1024

_NT = (((1,), (1,)), ((), ()))
_TN = (((0,), (0,)), ((), ()))


def _bucket_thresholds():
    max_exact = N_BUCKETS // 2
    thr = []
    for k in range(N_BUCKETS - max_exact):
        n = max_exact
        while min(max_exact + int(math.log(n / max_exact) / math.log(MAX_DISTANCE / max_exact)
                                  * (N_BUCKETS - max_exact)), N_BUCKETS - 1) < max_exact + k:
            n += 1
        thr.append(n)
    return thr


_BUCKET_THR = _bucket_thresholds()
N_NEAR = next(d for d in range(1, 1 << 20) if (d - 1) * MOBA_BLOCK + 1 >= _BUCKET_THR[-1])
N_NEAR_BLOCKS = N_NEAR
MOBA_TILE = 2 * MOBA_BLOCK
MOBA_NEAR_STEPS = N_NEAR_BLOCKS // 2 + 1


def _bias_kernel(rb_ref, out_ref):
    h = pl.program_id(0)
    n_heads = pl.num_programs(0)
    r = lax.broadcasted_iota(jnp.int32, (MOBA_BLOCK, MOBA_BLOCK), 0)
    c = lax.broadcasted_iota(jnp.int32, (MOBA_BLOCK, MOBA_BLOCK), 1)
    max_exact = N_BUCKETS // 2

    def rb(b):
        return rb_ref[b * n_heads + h] * LOG2E

    for d in range(N_NEAR_BLOCKS):
        n = d * MOBA_BLOCK + r - c
        nn = jnp.maximum(n, 0)
        val = jnp.full((MOBA_BLOCK, MOBA_BLOCK), rb(0), F32)
        for b in range(1, max_exact):
            val = jnp.where(nn >= b, rb(b), val)
        for k, thr in enumerate(_BUCKET_THR):
            val = jnp.where(nn >= thr, rb(max_exact + k), val)
        if d == 0:
            val = jnp.where(n >= 0, val, NEG_INF)
        out_ref[0, d] = val
    out_ref[0, N_NEAR_BLOCKS] = jnp.zeros((MOBA_BLOCK, MOBA_BLOCK), F32)


def _bias_tiles(rel_bias):
    n_heads = rel_bias.shape[1]
    return pl.pallas_call(
        _bias_kernel,
        grid=(n_heads,),
        in_specs=[pl.BlockSpec(memory_space=pltpu.SMEM)],
        out_specs=pl.BlockSpec((1, N_NEAR_BLOCKS + 1, MOBA_BLOCK, MOBA_BLOCK), lambda h: (h, 0, 0, 0)),
        out_shape=jax.ShapeDtypeStruct((n_heads, N_NEAR_BLOCKS + 1, MOBA_BLOCK, MOBA_BLOCK), F32),
        name="bias_tiles",
    )(rel_bias.astype(F32).reshape(-1))


def _cast_weights_once(pairs):
    @pl.when(pl.program_id(0) == 0)
    def _():
        for w_ref, w_scr in pairs:
            for r in range(0, w_ref.shape[0], CAST_ROWS):
                w_scr[r:r + CAST_ROWS, :] = w_ref[r:r + CAST_ROWS, :].astype(BF16)


def _inproj_kernel(x_ref, g_ref, w_ref, cos_ref, sin_ref, qg_ref, kg_ref,
                   dec_ref, dq_ref, dk_ref, cd_ref, rg_ref,
                   proj_ref, kmean_ref, ret_ref, w_scr, rin_scr, state_ref, *, seq_tiles):
    _cast_weights_once([(w_ref, w_scr)])
    x = x_ref[...]
    ms = jnp.mean(x * x, axis=-1, keepdims=True)
    h = (x * lax.rsqrt(ms + EPS) * g_ref[...]).astype(BF16)
    cos = cos_ref[...]
    sin = sin_ref[...]
    scale = HEAD_DIM ** -0.5
    rows = x.shape[0]
    heads = GROUP_WIDTH // HEAD_DIM

    def head_norm(y, gain):
        return y * lax.rsqrt(jnp.mean(y * y, axis=-1, keepdims=True) + EPS) * gain

    def section(s):
        return jnp.dot(h, w_scr[:, s * GROUP_WIDTH:(s + 1) * GROUP_WIDTH], preferred_element_type=F32)

    for s in range(RET_SECTIONS):
        y = section(s)
        for hh in range(heads):
            yh = y[:, hh * HEAD_DIM:(hh + 1) * HEAD_DIM]
            if s in (0, 1):
                yh = yh * cos + pltpu.roll(yh, HEAD_DIM // 2, axis=1) * sin
                if s == 1:
                    yh = yh * scale
            col = s * GROUP_WIDTH + hh * HEAD_DIM
            rin_scr[:, col:col + HEAD_DIM] = yh.astype(BF16)

    @pl.when(pl.program_id(0) % seq_tiles == 0)
    def _():
        state_ref[...] = jnp.zeros_like(state_ref)

    def moba_section(s):
        y = section(s)
        for hh in range(heads):
            yh = y[:, hh * HEAD_DIM:(hh + 1) * HEAD_DIM]
            if s == 4:
                yh = head_norm(yh, qg_ref[...]) * (scale * LOG2E)
            elif s == 5:
                yh = head_norm(yh, kg_ref[...])
                for blk in range(rows // MOBA_BLOCK):
                    kmean_ref[blk, :, hh * HEAD_DIM:(hh + 1) * HEAD_DIM] = jnp.mean(
                        yh[blk * MOBA_BLOCK:(blk + 1) * MOBA_BLOCK], axis=0, keepdims=True)
            col = (s - RET_SECTIONS) * GROUP_WIDTH + hh * HEAD_DIM
            proj_ref[:, col:col + HEAD_DIM] = yh.astype(BF16)

    chunk = RET_CHUNK
    units = [(c, hh) for c in range(rows // chunk) for hh in range(RET_HEADS)]

    def operand(c, hh, sec):
        return rin_scr[c * chunk:(c + 1) * chunk,
                       sec * GROUP_WIDTH + hh * HEAD_DIM:sec * GROUP_WIDTH + (hh + 1) * HEAD_DIM]

    masked, kv = {}, {}
    for c, hh in units:
        q, k, v = operand(c, hh, 0), operand(c, hh, 1), operand(c, hh, 2)
        sc = lax.dot_general(q, k, _NT, preferred_element_type=F32) * dec_ref[hh]
        masked[c, hh] = sc.astype(BF16)
        v_dec = (v.astype(F32) * dk_ref[hh]).astype(BF16)
        kv[c, hh] = lax.dot_general(k, v_dec, _TN, preferred_element_type=F32)
    before = {}
    for hh in range(RET_HEADS):
        state = state_ref[hh]
        for c in range(rows // chunk):
            before[c, hh] = state.astype(BF16)
            state = state * cd_ref[hh] + kv[c, hh]
        state_ref[hh] = state
    for c, hh in units:
        q, v = operand(c, hh, 0), operand(c, hh, 2)
        q_dec = (q.astype(F32) * dq_ref[hh]).astype(BF16)
        o = jnp.dot(jnp.concatenate([masked[c, hh], q_dec], axis=1),
                    jnp.concatenate([v, before[c, hh]], axis=0), preferred_element_type=F32)
        mu = jnp.mean(o, axis=-1, keepdims=True)
        oc = o - mu
        var = jnp.mean(oc * oc, axis=-1, keepdims=True)
        on = oc * lax.rsqrt(var + EPS) * rg_ref[:, hh * HEAD_DIM:(hh + 1) * HEAD_DIM]
        z = operand(c, hh, 3).astype(F32)
        ret_ref[c * chunk:(c + 1) * chunk, hh * HEAD_DIM:(hh + 1) * HEAD_DIM] = (
            z * jax.nn.sigmoid(z) * on).astype(BF16)

    for s in range(RET_SECTIONS, N_SECTIONS):
        moba_section(s)


def _retention_tables():
    c = RET_CHUNK
    log_g = np.log1p(-np.power(2.0, -5.0 - np.arange(RET_HEADS, dtype=np.float64)))
    j = np.arange(c, dtype=np.float64)
    rel = j[:, None] - j[None, :]
    decay = np.where(rel >= 0, np.exp(log_g[:, None, None] * np.maximum(rel, 0.0)[None]), 0.0)
    dq = np.exp(log_g[:, None] * (j + 1.0)[None])
    dk = np.exp(log_g[:, None] * (c - 1 - j)[None])
    cd = np.exp(log_g * c)
    wide = lambda t: np.broadcast_to(t[:, :, None], (RET_HEADS, c, HEAD_DIM))
    return (jnp.asarray(decay, F32), jnp.asarray(wide(dq), F32), jnp.asarray(wide(dk), F32),
            jnp.asarray(np.broadcast_to(cd[:, None, None], (RET_HEADS, 1, HEAD_DIM)), F32))


def _inproj(x2, layer, norm_g, w_in, cos, sin, q_g, k_g, ret_g, seq):
    m, d_model = x2.shape
    width = w_in.shape[2]
    moba_width = width - RET_SECTIONS * GROUP_WIDTH
    tm = ROW_TILE
    seq_tiles = seq // tm
    tables = _retention_tables()
    full = lambda a: pl.BlockSpec(a.shape, lambda i: (0,) * a.ndim)
    row1 = lambda n: pl.BlockSpec((None, 1, n), lambda i: (layer, 0, 0))
    return pl.pallas_call(
        functools.partial(_inproj_kernel, seq_tiles=seq_tiles),
        grid=(m // tm,),
        in_specs=[
            pl.BlockSpec((tm, d_model), lambda i: (i, 0)),
            row1(d_model),
            pl.BlockSpec((None, d_model, width), lambda i: (layer, 0, 0),
                         pipeline_mode=pl.Buffered(1)),
            pl.BlockSpec((tm, HEAD_DIM), lambda i: (i % seq_tiles, 0)),
            pl.BlockSpec((tm, HEAD_DIM), lambda i: (i % seq_tiles, 0)),
            row1(HEAD_DIM), row1(HEAD_DIM),
            *[full(t) for t in tables],
            row1(GROUP_WIDTH),
        ],
        out_specs=[
            pl.BlockSpec((tm, moba_width), lambda i: (i, 0)),
            pl.BlockSpec((tm // MOBA_BLOCK, 1, GROUP_WIDTH), lambda i: (i, 0, 0)),
            pl.BlockSpec((tm, GROUP_WIDTH), lambda i: (i, 0)),
        ],
        out_shape=[
            jax.ShapeDtypeStruct((m, moba_width), BF16),
            jax.ShapeDtypeStruct((m // MOBA_BLOCK, 1, GROUP_WIDTH), F32),
            jax.ShapeDtypeStruct((m, GROUP_WIDTH), BF16),
        ],
        scratch_shapes=[pltpu.VMEM((d_model, width), BF16),
                        pltpu.VMEM((tm, RET_SECTIONS * GROUP_WIDTH), BF16),
                        pltpu.VMEM((RET_HEADS, HEAD_DIM, HEAD_DIM), F32)],
        compiler_params=pltpu.CompilerParams(
            dimension_semantics=("arbitrary",), vmem_limit_bytes=VMEM_LIMIT),
        name="in_projection",
    )(x2, norm_g, w_in, cos, sin, q_g, k_g, *tables, ret_g)


def _moba_kernel(rb_ref, q_ref, k_ref, v_ref, z_ref, km_ref, oh_ref, bias_ref, o_ref,
                 qaug_scr, m_scr, acc_scr, s_scr):
    blk = MOBA_BLOCK
    tile = MOBA_TILE
    ti = pl.program_id(1)
    n_heads = MOBA_HEADS
    blk_id = lax.broadcasted_iota(jnp.int32, (SEL_LANES, tile), 0)
    blk_f = blk_id.astype(F32)
    q_blk = 2 * ti + lax.broadcasted_iota(jnp.int32, (SEL_LANES, tile), 1) // blk
    past = blk_id < q_blk
    far = (q_blk - blk_id) >= N_NEAR_BLOCKS

    for h in range(n_heads):
        cols = slice(h * HEAD_DIM, (h + 1) * HEAD_DIM)
        q = q_ref[:, cols]
        km = km_ref[0, :, cols]
        km_hi = km.astype(BF16)
        km_lo = (km - km_hi.astype(F32)).astype(BF16)
        gate = lax.dot_general(jnp.concatenate([km_hi, km_lo], axis=1),
                               jnp.concatenate([q, q], axis=1), _NT,
                               preferred_element_type=F32)
        g = jnp.where(past, gate, MASKED_GATE)
        sel = jnp.zeros(gate.shape, F32)
        for _ in range(MOBA_TOPK):
            top = jnp.max(g, axis=0, keepdims=True)
            first = jnp.min(jnp.where(g == top, blk_f, 1e9), axis=0, keepdims=True)
            pick = blk_f == first
            sel = jnp.where(pick, 1.0, sel)
            g = jnp.where(pick, MASKED_GATE, g)
        chosen = jnp.where(past, sel, 0.0) > 0.0
        far_c = jnp.full(gate.shape, rb_ref[(N_BUCKETS - 1) * n_heads + h] * LOG2E, F32)
        far_hi = far_c.astype(BF16).astype(F32)
        mask_hi = jnp.where(chosen, jnp.where(far, far_hi, 0.0), NEG_INF)
        mask_hi = jnp.where(blk_id == q_blk, 0.0, mask_hi)
        mask_lo = jnp.where(chosen, jnp.where(far, far_c - far_hi, 0.0), 0.0)
        mask_row = jnp.concatenate([mask_hi, mask_lo], axis=0).T
        qaug_scr[h] = jnp.concatenate([q, mask_row.astype(BF16)], axis=1)

    m_scr[...] = jnp.full(m_scr.shape, NEG_INF, F32)
    acc_scr[...] = jnp.zeros(acc_scr.shape, F32)
    ones = jnp.ones((tile, HEAD_DIM), BF16)

    def key_rows(step):
        return pl.ds(pl.multiple_of(jnp.maximum(ti - step, 0) * tile, tile), tile)

    def scores(step):
        rows = key_rows(step)
        onehot = oh_ref[rows, :]
        for h in range(n_heads):
            k_aug = jnp.concatenate([k_ref[rows, h * HEAD_DIM:(h + 1) * HEAD_DIM], onehot], axis=1)
            s_scr[step & 1, h] = lax.dot_general(qaug_scr[h], k_aug, _NT,
                                                 preferred_element_type=F32)

    def bias_tile(h, step):
        def part(offset):
            idx = jnp.where(offset < 0, N_NEAR_BLOCKS, jnp.minimum(offset, N_NEAR_BLOCKS))
            return bias_ref[h, idx]
        diag = part(2 * step)
        return jnp.concatenate([jnp.concatenate([diag, part(2 * step - 1)], axis=1),
                                jnp.concatenate([part(2 * step + 1), diag], axis=1)], axis=0)

    def accumulate(step, with_bias):
        rows = key_rows(step)
        for h in range(n_heads):
            v_aug = jnp.concatenate([v_ref[rows, h * HEAD_DIM:(h + 1) * HEAD_DIM], ones], axis=1)
            s = s_scr[step & 1, h]
            if with_bias:
                s = s + bias_tile(h, step)
            m_prev = m_scr[h]
            m_next = jnp.maximum(m_prev, jnp.max(s, axis=1, keepdims=True))
            alpha = jnp.exp2(m_prev - m_next)
            p = jnp.exp2(s - jnp.concatenate([m_next] * (tile // HEAD_DIM), axis=1))
            acc_scr[h] = (jnp.concatenate([alpha, alpha], axis=1) * acc_scr[h]
                          + jnp.dot(p.astype(BF16), v_aug, preferred_element_type=F32))
            m_scr[h] = m_next

    def near_body(step, carry):
        accumulate(step, True)
        scores(step + 1)
        return carry

    def far_body(step, carry):
        accumulate(step, False)
        scores(step + 1)
        return carry

    n_near = jnp.minimum(ti + 1, MOBA_NEAR_STEPS)
    scores(0)
    lax.fori_loop(0, n_near, near_body, 0)
    lax.fori_loop(n_near, ti + 1, far_body, 0)
    for h in range(n_heads):
        cols = slice(h * HEAD_DIM, (h + 1) * HEAD_DIM)
        z = z_ref[:, cols].astype(F32)
        out = acc_scr[h, :, :HEAD_DIM] / acc_scr[h, :, HEAD_DIM:]
        o_ref[:, cols] = (z * jax.nn.sigmoid(z) * out).astype(BF16)


def _moba(proj, kmean, bias_tiles, rel_bias, batch, seq):
    m = proj.shape[0]
    tile = MOBA_TILE
    nq = seq // tile
    row_blk = np.arange(seq)[:, None] // MOBA_BLOCK
    onehot = jnp.asarray((np.arange(HEAD_DIM)[None, :] & (SEL_LANES - 1)) == row_blk, BF16)
    rows = lambda sec: pl.BlockSpec((tile, GROUP_WIDTH), lambda b, i: (b * nq + i, sec))
    whole = lambda sec: pl.BlockSpec((seq, GROUP_WIDTH), lambda b, i: (b, sec),
                                     pipeline_mode=pl.Buffered(1))
    return pl.pallas_call(
        _moba_kernel,
        grid=(batch, nq),
        in_specs=[pl.BlockSpec(memory_space=pltpu.SMEM),
                  rows(0), whole(1), whole(2), rows(3),
                  pl.BlockSpec((1, SEL_LANES, GROUP_WIDTH), lambda b, i: (b, 0, 0)),
                  pl.BlockSpec(onehot.shape, lambda b, i: (0, 0), pipeline_mode=pl.Buffered(1)),
                  pl.BlockSpec(bias_tiles.shape, lambda b, i: (0, 0, 0, 0),
                               pipeline_mode=pl.Buffered(1))],
        out_specs=pl.BlockSpec((tile, GROUP_WIDTH), lambda b, i: (b * nq + i, 0)),
        out_shape=jax.ShapeDtypeStruct((m, GROUP_WIDTH), BF16),
        scratch_shapes=[pltpu.VMEM((MOBA_HEADS, tile, 2 * HEAD_DIM), BF16),
                        pltpu.VMEM((MOBA_HEADS, tile, HEAD_DIM), F32),
                        pltpu.VMEM((MOBA_HEADS, tile, 2 * HEAD_DIM), F32),
                        pltpu.VMEM((2, MOBA_HEADS, tile, tile), F32)],
        compiler_params=pltpu.CompilerParams(
            dimension_semantics=("arbitrary", "arbitrary"), vmem_limit_bytes=VMEM_LIMIT),
        name="moba",
    )(rel_bias.astype(F32).reshape(-1), proj, proj, proj, proj, kmean, onehot, bias_tiles)


def _outproj_kernel(x_ref, r_ref, a_ref, wo_ref, p_ref, wp_ref, pg_ref, wg_ref, o_ref,
                    wo_scr, wp_scr, wg_scr):
    _cast_weights_once([(wo_ref, wo_scr), (wp_ref, wp_scr), (wg_ref, wg_scr)])

    def unit_rms(t):
        return t * lax.rsqrt(jnp.mean(t * t, axis=-1, keepdims=True) + EPS)

    y = (jnp.dot(r_ref[...], wo_scr[:GROUP_WIDTH, :], preferred_element_type=F32)
         + jnp.dot(a_ref[...], wo_scr[GROUP_WIDTH:, :], preferred_element_type=F32))
    x1 = x_ref[...] + y
    gate = jax.nn.sigmoid(jnp.dot(unit_rms(x1).astype(BF16), wg_scr[...],
                                  preferred_element_type=F32))
    e = unit_rms(jnp.dot(p_ref[...].astype(BF16), wp_scr[...], preferred_element_type=F32))
    o_ref[...] = x1 + gate * (e * pg_ref[...])


def _outproj(x2, r_out, m_out, layer, w_out, p3, w_ple, ple_g, w_gate):
    m, d_model = x2.shape
    tm = ROW_TILE
    rows = lambda a: pl.BlockSpec((tm, a.shape[1]), lambda i: (i, 0))
    full = lambda a: pl.BlockSpec((None,) + a.shape[1:], lambda i: (layer, 0, 0),
                                  pipeline_mode=pl.Buffered(1))
    return pl.pallas_call(
        _outproj_kernel,
        grid=(m // tm,),
        in_specs=[rows(x2), rows(r_out), rows(m_out), full(w_out),
                  pl.BlockSpec((None, tm, p3.shape[2]), lambda i: (layer, i, 0)), full(w_ple),
                  full(ple_g), full(w_gate)],
        out_specs=pl.BlockSpec((tm, d_model), lambda i: (i, 0)),
        out_shape=jax.ShapeDtypeStruct((m, d_model), F32),
        scratch_shapes=[pltpu.VMEM(w.shape[1:], BF16) for w in (w_out, w_ple, w_gate)],
        compiler_params=pltpu.CompilerParams(
            dimension_semantics=("arbitrary",), vmem_limit_bytes=VMEM_LIMIT),
        name="out_projection",
    )(x2, r_out, m_out, w_out, p3, w_ple, ple_g, w_gate)


def _rotary_tables(seq):
    half = HEAD_DIM // 2
    inv = np.float32(ROPE_THETA) ** (-np.arange(half, dtype=np.float32) / np.float32(half))
    ang = np.arange(seq, dtype=np.float32)[:, None] * inv[None, :]
    cos, sin = np.cos(ang), np.sin(ang)
    return (jnp.asarray(np.concatenate([cos, cos], axis=1), F32),
            jnp.asarray(np.concatenate([-sin, sin], axis=1), F32))


def kernel(x, p, norm_g, w_in, ret_norm_g, q_norm_g, k_norm_g, rel_bias, w_out, w_ple, ple_norm_g, w_ple_gate):
    batch, seq, d_model = x.shape
    depth = w_in.shape[0]
    m = batch * seq
    assert w_in.shape[2] == N_SECTIONS * GROUP_WIDTH and w_out.shape[1] == 2 * GROUP_WIDTH
    assert rel_bias.shape == (N_BUCKETS, MOBA_HEADS)
    assert seq % ROW_TILE == 0 and ROW_TILE % MOBA_BLOCK == 0 and seq % MOBA_TILE == 0
    assert ROW_TILE % RET_CHUNK == 0 and seq // MOBA_BLOCK <= SEL_LANES

    cos, sin = _rotary_tables(seq)
    bias_tiles = _bias_tiles(rel_bias)
    x2 = x.reshape(m, d_model)
    nb = seq // MOBA_BLOCK
    stack = lambda g: g.reshape(depth, 1, -1)
    p3 = p.reshape(depth, m, -1)
    for i in range(depth):
        proj, kmean, r_out = _inproj(x2, i, stack(norm_g), w_in, cos, sin,
                                     stack(q_norm_g), stack(k_norm_g), stack(ret_norm_g), seq)
        kmean = jnp.pad(kmean.reshape(batch, nb, GROUP_WIDTH), ((0, 0), (0, SEL_LANES - nb), (0, 0)))
        m_out = _moba(proj, kmean, bias_tiles, rel_bias, batch, seq)
        x2 = _outproj(x2, r_out, m_out, i, w_out, p3, w_ple, stack(ple_norm_g), w_ple_gate)
    return x2.reshape(batch, seq, d_model)
```

```python
import functools
import math

import numpy as np
import jax
import jax.numpy as jnp
from jax import lax
from jax.experimental import pallas as pl
from jax.experimental.pallas import tpu as pltpu

F32 = jnp.float32
BF16 = jnp.bfloat16

HEAD_DIM = 128
RET_HEADS = 4
MOBA_HEADS = 4
GROUP_WIDTH = 512
N_SECTIONS = 8
RET_SECTIONS = 4
RET_CHUNK = 128
MOBA_BLOCK = 256
MOBA_TOPK = 3
N_BUCKETS = 32
MAX_DISTANCE = 2048
ROPE_THETA = 10000.0
EPS = 1e-6
NEG_INF = -1e30
MASKED_GATE = -3e38
LOG2E = math.log2(math.e)
SEL_LANES = HEAD_DIM // 2

ROW_TILE = 512
OUT_ROW_TILE = 1024
CAST_ROWS = 256
VMEM_LIMIT = 56 * 1024 * 1024

_NT = (((1,), (1,)), ((), ()))
_TN = (((0,), (0,)), ((), ()))


def _bucket_thresholds():
    max_exact = N_BUCKETS // 2
    thr = []
    for k in range(N_BUCKETS - max_exact):
        n = max_exact
        while min(max_exact + int(math.log(n / max_exact) / math.log(MAX_DISTANCE / max_exact)
                                  * (N_BUCKETS - max_exact)), N_BUCKETS - 1) < max_exact + k:
            n += 1
        thr.append(n)
    return thr


_BUCKET_THR = _bucket_thresholds()
N_NEAR = next(d for d in range(1, 1 << 20) if (d - 1) * MOBA_BLOCK + 1 >= _BUCKET_THR[-1])
N_NEAR_BLOCKS = N_NEAR
MOBA_TILE = 2 * MOBA_BLOCK
MOBA_NEAR_STEPS = N_NEAR_BLOCKS // 2 + 1


def _bias_kernel(rb_ref, out_ref):
    h = pl.program_id(0)
    n_heads = pl.num_programs(0)
    r = lax.broadcasted_iota(jnp.int32, (MOBA_BLOCK, MOBA_BLOCK), 0)
    c = lax.broadcasted_iota(jnp.int32, (MOBA_BLOCK, MOBA_BLOCK), 1)
    max_exact = N_BUCKETS // 2

    def rb(b):
        return rb_ref[b * n_heads + h] * LOG2E

    for d in range(N_NEAR_BLOCKS):
        n = d * MOBA_BLOCK + r - c
        nn = jnp.maximum(n, 0)
        val = jnp.full((MOBA_BLOCK, MOBA_BLOCK), rb(0), F32)
        for b in range(1, max_exact):
            val = jnp.where(nn >= b, rb(b), val)
        for k, thr in enumerate(_BUCKET_THR):
            val = jnp.where(nn >= thr, rb(max_exact + k), val)
        if d == 0:
            val = jnp.where(n >= 0, val, NEG_INF)
        out_ref[0, d] = val
    out_ref[0, N_NEAR_BLOCKS] = jnp.zeros((MOBA_BLOCK, MOBA_BLOCK), F32)


def _bias_tiles(rel_bias):
    n_heads = rel_bias.shape[1]
    return pl.pallas_call(
        _bias_kernel,
        grid=(n_heads,),
        in_specs=[pl.BlockSpec(memory_space=pltpu.SMEM)],
        out_specs=pl.BlockSpec((1, N_NEAR_BLOCKS + 1, MOBA_BLOCK, MOBA_BLOCK), lambda h: (h, 0, 0, 0)),
        out_shape=jax.ShapeDtypeStruct((n_heads, N_NEAR_BLOCKS + 1, MOBA_BLOCK, MOBA_BLOCK), F32),
        name="bias_tiles",
    )(rel_bias.astype(F32).reshape(-1))


def _cast_weights_once(pairs):
    @pl.when(pl.program_id(0) == 0)
    def _():
        for w_ref, w_scr in pairs:
            for r in range(0, w_ref.shape[0], CAST_ROWS):
                w_scr[r:r + CAST_ROWS, :] = w_ref[r:r + CAST_ROWS, :].astype(BF16)


def _inproj_kernel(x_ref, g_ref, w_ref, cos_ref, sin_ref, qg_ref, kg_ref,
                   dec_ref, dq_ref, dk_ref, cd_ref, rg_ref,
                   proj_ref, kmean_ref, ret_ref, w_scr, rin_scr, state_ref, *, seq_tiles):
    _cast_weights_once([(w_ref, w_scr)])
    x = x_ref[...]
    ms = jnp.mean(x * x, axis=-1, keepdims=True)
    h = (x * lax.rsqrt(ms + EPS) * g_ref[...]).astype(BF16)
    cos = cos_ref[...]
    sin = sin_ref[...]
    scale = HEAD_DIM ** -0.5
    rows = x.shape[0]
    heads = GROUP_WIDTH // HEAD_DIM

    def head_norm(y, gain):
        return y * lax.rsqrt(jnp.mean(y * y, axis=-1, keepdims=True) + EPS) * gain

    def section(s):
        return jnp.dot(h, w_scr[:, s * GROUP_WIDTH:(s + 1) * GROUP_WIDTH], preferred_element_type=F32)

    for s in range(RET_SECTIONS):
        y = section(s)
        for hh in range(heads):
            yh = y[:, hh * HEAD_DIM:(hh + 1) * HEAD_DIM]
            if s in (0, 1):
                yh = yh * cos + pltpu.roll(yh, HEAD_DIM // 2, axis=1) * sin
                if s == 1:
                    yh = yh * scale
            col = s * GROUP_WIDTH + hh * HEAD_DIM
            rin_scr[:, col:col + HEAD_DIM] = yh.astype(BF16)

    @pl.when(pl.program_id(0) % seq_tiles == 0)
    def _():
        state_ref[...] = jnp.zeros_like(state_ref)

    def moba_section(s):
        y = section(s)
        for hh in range(heads):
            yh = y[:, hh * HEAD_DIM:(hh + 1) * HEAD_DIM]
            if s == 4:
                yh = head_norm(yh, qg_ref[...]) * (scale * LOG2E)
            elif s == 5:
                yh = head_norm(yh, kg_ref[...])
                for blk in range(rows // MOBA_BLOCK):
                    kmean_ref[blk, :, hh * HEAD_DIM:(hh + 1) * HEAD_DIM] = jnp.mean(
                        yh[blk * MOBA_BLOCK:(blk + 1) * MOBA_BLOCK], axis=0, keepdims=True)
            col = (s - RET_SECTIONS) * GROUP_WIDTH + hh * HEAD_DIM
            proj_ref[:, col:col + HEAD_DIM] = yh.astype(BF16)

    chunk = RET_CHUNK
    units = [(c, hh) for c in range(rows // chunk) for hh in range(RET_HEADS)]

    def operand(c, hh, sec):
        return rin_scr[c * chunk:(c + 1) * chunk,
                       sec * GROUP_WIDTH + hh * HEAD_DIM:sec * GROUP_WIDTH + (hh + 1) * HEAD_DIM]

    masked, kv = {}, {}
    for c, hh in units:
        q, k, v = operand(c, hh, 0), operand(c, hh, 1), operand(c, hh, 2)
        sc = lax.dot_general(q, k, _NT, preferred_element_type=F32) * dec_ref[hh]
        masked[c, hh] = sc.astype(BF16)
        v_dec = (v.astype(F32) * dk_ref[hh]).astype(BF16)
        kv[c, hh] = lax.dot_general(k, v_dec, _TN, preferred_element_type=F32)
    before = {}
    for hh in range(RET_HEADS):
        state = state_ref[hh]
        for c in range(rows // chunk):
            before[c, hh] = state.astype(BF16)
            state = state * cd_ref[hh] + kv[c, hh]
        state_ref[hh] = state
    for c, hh in units:
        q, v = operand(c, hh, 0), operand(c, hh, 2)
        q_dec = (q.astype(F32) * dq_ref[hh]).astype(BF16)
        o = jnp.dot(jnp.concatenate([masked[c, hh], q_dec], axis=1),
                    jnp.concatenate([v, before[c, hh]], axis=0), preferred_element_type=F32)
        mu = jnp.mean(o, axis=-1, keepdims=True)
        oc = o - mu
        var = jnp.mean(oc * oc, axis=-1, keepdims=True)
        on = oc * lax.rsqrt(var + EPS) * rg_ref[:, hh * HEAD_DIM:(hh + 1) * HEAD_DIM]
        z = operand(c, hh, 3).astype(F32)
        ret_ref[c * chunk:(c + 1) * chunk, hh * HEAD_DIM:(hh + 1) * HEAD_DIM] = (
            z * jax.nn.sigmoid(z) * on).astype(BF16)

    for s in range(RET_SECTIONS, N_SECTIONS):
        moba_section(s)


def _retention_tables():
    c = RET_CHUNK
    log_g = np.log1p(-np.power(2.0, -5.0 - np.arange(RET_HEADS, dtype=np.float64)))
    j = np.arange(c, dtype=np.float64)
    rel = j[:, None] - j[None, :]
    decay = np.where(rel >= 0, np.exp(log_g[:, None, None] * np.maximum(rel, 0.0)[None]), 0.0)
    dq = np.exp(log_g[:, None] * (j + 1.0)[None])
    dk = np.exp(log_g[:, None] * (c - 1 - j)[None])
    cd = np.exp(log_g * c)
    wide = lambda t: np.broadcast_to(t[:, :, None], (RET_HEADS, c, HEAD_DIM))
    return (jnp.asarray(decay, F32), jnp.asarray(wide(dq), F32), jnp.asarray(wide(dk), F32),
            jnp.asarray(np.broadcast_to(cd[:, None, None], (RET_HEADS, 1, HEAD_DIM)), F32))


def _inproj(x2, layer, norm_g, w_in, cos, sin, q_g, k_g, ret_g, seq):
    m, d_model = x2.shape
    width = w_in.shape[2]
    moba_width = width - RET_SECTIONS * GROUP_WIDTH
    tm = ROW_TILE
    seq_tiles = seq // tm
    tables = _retention_tables()
    full = lambda a: pl.BlockSpec(a.shape, lambda i: (0,) * a.ndim)
    row1 = lambda n: pl.BlockSpec((None, 1, n), lambda i: (layer, 0, 0))
    return pl.pallas_call(
        functools.partial(_inproj_kernel, seq_tiles=seq_tiles),
        grid=(m // tm,),
        in_specs=[
            pl.BlockSpec((tm, d_model), lambda i: (i, 0)),
            row1(d_model),
            pl.BlockSpec((None, d_model, width), lambda i: (layer, 0, 0),
                         pipeline_mode=pl.Buffered(1)),
            pl.BlockSpec((tm, HEAD_DIM), lambda i: (i % seq_tiles, 0)),
            pl.BlockSpec((tm, HEAD_DIM), lambda i: (i % seq_tiles, 0)),
            row1(HEAD_DIM), row1(HEAD_DIM),
            *[full(t) for t in tables],
            row1(GROUP_WIDTH),
        ],
        out_specs=[
            pl.BlockSpec((tm, moba_width), lambda i: (i, 0)),
            pl.BlockSpec((tm // MOBA_BLOCK, 1, GROUP_WIDTH), lambda i: (i, 0, 0)),
            pl.BlockSpec((tm, GROUP_WIDTH), lambda i: (i, 0)),
        ],
        out_shape=[
            jax.ShapeDtypeStruct((m, moba_width), BF16),
            jax.ShapeDtypeStruct((m // MOBA_BLOCK, 1, GROUP_WIDTH), F32),
            jax.ShapeDtypeStruct((m, GROUP_WIDTH), BF16),
        ],
        scratch_shapes=[pltpu.VMEM((d_model, width), BF16),
                        pltpu.VMEM((tm, RET_SECTIONS * GROUP_WIDTH), BF16),
                        pltpu.VMEM((RET_HEADS, HEAD_DIM, HEAD_DIM), F32)],
        compiler_params=pltpu.CompilerParams(
            dimension_semantics=("arbitrary",), vmem_limit_bytes=VMEM_LIMIT),
        name="in_projection",
    )(x2, norm_g, w_in, cos, sin, q_g, k_g, *tables, ret_g)


def _moba_kernel(rb_ref, q_ref, k_ref, v_ref, z_ref, km_ref, oh_ref, bias_ref, o_ref,
                 qaug_scr, m_scr, acc_scr, s_scr):
    blk = MOBA_BLOCK
    tile = MOBA_TILE
    ti = pl.program_id(1)
    n_heads = MOBA_HEADS
    blk_id = lax.broadcasted_iota(jnp.int32, (SEL_LANES, tile), 0)
    blk_f = blk_id.astype(F32)
    q_blk = 2 * ti + lax.broadcasted_iota(jnp.int32, (SEL_LANES, tile), 1) // blk
    past = blk_id < q_blk
    far = (q_blk - blk_id) >= N_NEAR_BLOCKS

    for h in range(n_heads):
        cols = slice(h * HEAD_DIM, (h + 1) * HEAD_DIM)
        q = q_ref[:, cols]
        km = km_ref[0, :, cols]
        km_hi = km.astype(BF16)
        km_lo = (km - km_hi.astype(F32)).astype(BF16)
        gate = lax.dot_general(jnp.concatenate([km_hi, km_lo], axis=1),
                               jnp.concatenate([q, q], axis=1), _NT,
                               preferred_element_type=F32)
        g = jnp.where(past, gate, MASKED_GATE)
        sel = jnp.zeros(gate.shape, F32)
        for _ in range(MOBA_TOPK):
            top = jnp.max(g, axis=0, keepdims=True)
            first = jnp.min(jnp.where(g == top, blk_f, 1e9), axis=0, keepdims=True)
            pick = blk_f == first
            sel = jnp.where(pick, 1.0, sel)
            g = jnp.where(pick, MASKED_GATE, g)
        chosen = jnp.where(past, sel, 0.0) > 0.0
        far_c = jnp.full(gate.shape, rb_ref[(N_BUCKETS - 1) * n_heads + h] * LOG2E, F32)
        far_hi = far_c.astype(BF16).astype(F32)
        mask_hi = jnp.where(chosen, jnp.where(far, far_hi, 0.0), NEG_INF)
        mask_hi = jnp.where(blk_id == q_blk, 0.0, mask_hi)
        mask_lo = jnp.where(chosen, jnp.where(far, far_c - far_hi, 0.0), 0.0)
        mask_row = jnp.concatenate([mask_hi, mask_lo], axis=0).T
        qaug_scr[h] = jnp.concatenate([q, mask_row.astype(BF16)], axis=1)

    m_scr[...] = jnp.full(m_scr.shape, NEG_INF, F32)
    acc_scr[...] = jnp.zeros(acc_scr.shape, F32)
    ones = jnp.ones((tile, HEAD_DIM), BF16)

    def key_rows(step):
        return pl.ds(pl.multiple_of((ti - step) * tile, tile), tile)

    def scores(step):
        rows = key_rows(step)
        onehot = oh_ref[rows, :]
        for h in range(n_heads):
            k_aug = jnp.concatenate([k_ref[rows, h * HEAD_DIM:(h + 1) * HEAD_DIM], onehot], axis=1)
            s_scr[step & 1, h] = lax.dot_general(qaug_scr[h], k_aug, _NT,
                                                 preferred_element_type=F32)

    def bias_tile(h, step):
        def part(offset):
            idx = jnp.where(offset < 0, N_NEAR_BLOCKS, jnp.minimum(offset, N_NEAR_BLOCKS))
            return bias_ref[h, idx]
        diag = part(2 * step)
        return jnp.concatenate([jnp.concatenate([diag, part(2 * step - 1)], axis=1),
                                jnp.concatenate([part(2 * step + 1), diag], axis=1)], axis=0)

    def accumulate(step, with_bias):
        rows = key_rows(step)
        for h in range(n_heads):
            v_aug = jnp.concatenate([v_ref[rows, h * HEAD_DIM:(h + 1) * HEAD_DIM], ones], axis=1)
            s = s_scr[step & 1, h]
            if with_bias:
                s = s + bias_tile(h, step)
            m_prev = m_scr[h]
            m_next = jnp.maximum(m_prev, jnp.max(s, axis=1, keepdims=True))
            alpha = jnp.exp2(m_prev - m_next)
            p = jnp.exp2(s - jnp.concatenate([m_next] * (tile // HEAD_DIM), axis=1))
            acc_scr[h] = (jnp.concatenate([alpha, alpha], axis=1) * acc_scr[h]
                          + jnp.dot(p.astype(BF16), v_aug, preferred_element_type=F32))
            m_scr[h] = m_next

    def loop_body(with_bias):
        def body(step, carry):
            accumulate(step, with_bias)
            scores(step + 1)
            return carry
        return body

    n_near = jnp.minimum(ti, MOBA_NEAR_STEPS)
    scores(0)
    lax.fori_loop(0, n_near, loop_body(True), 0)
    lax.fori_loop(n_near, ti, loop_body(False), 0)
    pl.when(ti < MOBA_NEAR_STEPS)(functools.partial(accumulate, ti, True))
    pl.when(ti >= MOBA_NEAR_STEPS)(functools.partial(accumulate, ti, False))
    for h in range(n_heads):
        cols = slice(h * HEAD_DIM, (h + 1) * HEAD_DIM)
        z = z_ref[:, cols].astype(F32)
        out = acc_scr[h, :, :HEAD_DIM] / acc_scr[h, :, HEAD_DIM:]
        o_ref[:, cols] = (z * jax.nn.sigmoid(z) * out).astype(BF16)


def _moba(proj, kmean, bias_tiles, rel_bias, batch, seq):
    m = proj.shape[0]
    tile = MOBA_TILE
    nq = seq // tile
    row_blk = np.arange(seq)[:, None] // MOBA_BLOCK
    onehot = jnp.asarray((np.arange(HEAD_DIM)[None, :] & (SEL_LANES - 1)) == row_blk, BF16)
    rows = lambda sec: pl.BlockSpec((tile, GROUP_WIDTH), lambda b, i: (b * nq + i, sec))
    whole = lambda sec: pl.BlockSpec((seq, GROUP_WIDTH), lambda b, i: (b, sec),
                                     pipeline_mode=pl.Buffered(1))
    return pl.pallas_call(
        _moba_kernel,
        grid=(batch, nq),
        in_specs=[pl.BlockSpec(memory_space=pltpu.SMEM),
                  rows(0), whole(1), whole(2), rows(3),
                  pl.BlockSpec((1, SEL_LANES, GROUP_WIDTH), lambda b, i: (b, 0, 0)),
                  pl.BlockSpec(onehot.shape, lambda b, i: (0, 0), pipeline_mode=pl.Buffered(1)),
                  pl.BlockSpec(bias_tiles.shape, lambda b, i: (0, 0, 0, 0),
                               pipeline_mode=pl.Buffered(1))],
        out_specs=pl.BlockSpec((tile, GROUP_WIDTH), lambda b, i: (b * nq + i, 0)),
        out_shape=jax.ShapeDtypeStruct((m, GROUP_WIDTH), BF16),
        scratch_shapes=[pltpu.VMEM((MOBA_HEADS, tile, 2 * HEAD_DIM), BF16),
                        pltpu.VMEM((MOBA_HEADS, tile, HEAD_DIM), F32),
                        pltpu.VMEM((MOBA_HEADS, tile, 2 * HEAD_DIM), F32),
                        pltpu.VMEM((2, MOBA_HEADS, tile, tile), F32)],
        compiler_params=pltpu.CompilerParams(
            dimension_semantics=("arbitrary", "arbitrary"), vmem_limit_bytes=VMEM_LIMIT),
        name="moba",
    )(rel_bias.astype(F32).reshape(-1), proj, proj, proj, proj, kmean, onehot, bias_tiles)


def _outproj_kernel(x_ref, r_ref, a_ref, wo_ref, p_ref, wp_ref, pg_ref, wg_ref, o_ref,
                    wo_scr, wp_scr, wg_scr):
    _cast_weights_once([(wo_ref, wo_scr), (wp_ref, wp_scr), (wg_ref, wg_scr)])

    def unit_rms(t):
        return t * lax.rsqrt(jnp.mean(t * t, axis=-1, keepdims=True) + EPS)

    half = x_ref.shape[0] // 2
    halves = [slice(0, half), slice(half, 2 * half)]
    for rows in halves:
        y = (jnp.dot(r_ref[rows, :], wo_scr[:GROUP_WIDTH, :], preferred_element_type=F32)
             + jnp.dot(a_ref[rows, :], wo_scr[GROUP_WIDTH:, :], preferred_element_type=F32))
        o_ref[rows, :] = x_ref[rows, :] + y
    for rows in halves:
        e = unit_rms(jnp.dot(p_ref[rows, :].astype(BF16), wp_scr[...], preferred_element_type=F32))
        x1 = o_ref[rows, :]
        gate = jax.nn.sigmoid(jnp.dot(unit_rms(x1).astype(BF16), wg_scr[...],
                                      preferred_element_type=F32))
        o_ref[rows, :] = x1 + gate * (e * pg_ref[...])


def _outproj(x2, r_out, m_out, layer, w_out, p3, w_ple, ple_g, w_gate):
    m, d_model = x2.shape
    tm = OUT_ROW_TILE
    rows = lambda a: pl.BlockSpec((tm, a.shape[1]), lambda i: (i, 0))
    full = lambda a: pl.BlockSpec((None,) + a.shape[1:], lambda i: (layer, 0, 0),
                                  pipeline_mode=pl.Buffered(1))
    return pl.pallas_call(
        _outproj_kernel,
        grid=(m // tm,),
        in_specs=[rows(x2), rows(r_out), rows(m_out), full(w_out),
                  pl.BlockSpec((None, tm, p3.shape[2]), lambda i: (layer, i, 0)), full(w_ple),
                  full(ple_g), full(w_gate)],
        out_specs=pl.BlockSpec((tm, d_model), lambda i: (i, 0)),
        out_shape=jax.ShapeDtypeStruct((m, d_model), F32),
        scratch_shapes=[pltpu.VMEM(w.shape[1:], BF16) for w in (w_out, w_ple, w_gate)],
        compiler_params=pltpu.CompilerParams(
            dimension_semantics=("arbitrary",), vmem_limit_bytes=VMEM_LIMIT),
        name="out_projection",
    )(x2, r_out, m_out, w_out, p3, w_ple, ple_g, w_gate)


def _rotary_tables(seq):
    half = HEAD_DIM // 2
    inv = np.float32(ROPE_THETA) ** (-np.arange(half, dtype=np.float32) / np.float32(half))
    ang = np.arange(seq, dtype=np.float32)[:, None] * inv[None, :]
    cos, sin = np.cos(ang), np.sin(ang)
    return (jnp.asarray(np.concatenate([cos, cos], axis=1), F32),
            jnp.asarray(np.concatenate([-sin, sin], axis=1), F32))


def kernel(x, p, norm_g, w_in, ret_norm_g, q_norm_g, k_norm_g, rel_bias, w_out, w_ple, ple_norm_g, w_ple_gate):
    batch, seq, d_model = x.shape
    depth = w_in.shape[0]
    m = batch * seq
    assert w_in.shape[2] == N_SECTIONS * GROUP_WIDTH and w_out.shape[1] == 2 * GROUP_WIDTH
    assert rel_bias.shape == (N_BUCKETS, MOBA_HEADS)
    assert seq % ROW_TILE == 0 and ROW_TILE % MOBA_BLOCK == 0 and seq % MOBA_TILE == 0
    assert m % OUT_ROW_TILE == 0
    assert ROW_TILE % RET_CHUNK == 0 and seq // MOBA_BLOCK <= SEL_LANES

    cos, sin = _rotary_tables(seq)
    bias_tiles = _bias_tiles(rel_bias)
    x2 = x.reshape(m, d_model)
    nb = seq // MOBA_BLOCK
    stack = lambda g: g.reshape(depth, 1, -1)
    p3 = p.reshape(depth, m, -1)
    for i in range(depth):
        proj, kmean, r_out = _inproj(x2, i, stack(norm_g), w_in, cos, sin,
                                     stack(q_norm_g), stack(k_norm_g), stack(ret_norm_g), seq)
        kmean = jnp.pad(kmean.reshape(batch, nb, GROUP_WIDTH), ((0, 0), (0, SEL_LANES - nb), (0, 0)))
        m_out = _moba(proj, kmean, bias_tiles, rel_bias, batch, seq)
        x2 = _outproj(x2, r_out, m_out, i, w_out, p3, w_ple, stack(ple_norm_g), w_ple_gate)
    return x2.reshape(batch, seq, d_model)
```

```python
import functools
import math

import numpy as np
import jax
import jax.numpy as jnp
from jax import lax
from jax.experimental import pallas as pl
from jax.experimental.pallas import tpu as pltpu

F32 = jnp.float32
BF16 = jnp.bfloat16

HEAD_DIM = 128
SUBLANES = 8
RET_HEADS = 4
MOBA_HEADS = 4
GROUP_WIDTH = 512
N_SECTIONS = 8
RET_SECTIONS = 4
MOBA_V_SECTION = 6
MOBA_PROJ_SECTIONS = (4, 5, 7)
RET_CHUNK = 128
MOBA_BLOCK = 256
MOBA_TOPK = 3
N_BUCKETS = 32
MAX_DISTANCE = 2048
ROPE_THETA = 10000.0
EPS = 1e-6
NEG_INF = -1e30
MASKED_GATE = -3e38
LOG2E = math.log2(math.e)
SEL_LANES = HEAD_DIM // 2

ROW_TILE = 512
OUT_ROW_TILE = 1024
CAST_ROWS = 256
VMEM_LIMIT = 56 * 1024 * 1024

_NT = (((1,), (1,)), ((), ()))
_TN = (((0,), (0,)), ((), ()))


def _bucket_thresholds():
    max_exact = N_BUCKETS // 2
    thr = []
    for k in range(N_BUCKETS - max_exact):
        n = max_exact
        while min(max_exact + int(math.log(n / max_exact) / math.log(MAX_DISTANCE / max_exact)
                                  * (N_BUCKETS - max_exact)), N_BUCKETS - 1) < max_exact + k:
            n += 1
        thr.append(n)
    return thr


_BUCKET_THR = _bucket_thresholds()
N_NEAR = next(d for d in range(1, 1 << 20) if (d - 1) * MOBA_BLOCK + 1 >= _BUCKET_THR[-1])
N_NEAR_BLOCKS = N_NEAR
MOBA_TILE = 2 * MOBA_BLOCK
MOBA_NEAR_STEPS = N_NEAR_BLOCKS // 2 + 1


def _bias_kernel(rb_ref, out_ref):
    h = pl.program_id(0)
    n_heads = pl.num_programs(0)
    c = lax.broadcasted_iota(jnp.int32, (MOBA_BLOCK, MOBA_BLOCK), 0)
    r = lax.broadcasted_iota(jnp.int32, (MOBA_BLOCK, MOBA_BLOCK), 1)
    max_exact = N_BUCKETS // 2

    def rb(b):
        return rb_ref[b * n_heads + h] * LOG2E

    for d in range(N_NEAR_BLOCKS):
        n = d * MOBA_BLOCK + r - c
        nn = jnp.maximum(n, 0)
        val = jnp.full((MOBA_BLOCK, MOBA_BLOCK), rb(0), F32)
        for b in range(1, max_exact):
            val = jnp.where(nn >= b, rb(b), val)
        for k, thr in enumerate(_BUCKET_THR):
            val = jnp.where(nn >= thr, rb(max_exact + k), val)
        if d == 0:
            val = jnp.where(n >= 0, val, NEG_INF)
        out_ref[0, d] = val
    out_ref[0, N_NEAR_BLOCKS] = jnp.zeros((MOBA_BLOCK, MOBA_BLOCK), F32)


def _bias_tiles(rel_bias):
    n_heads = rel_bias.shape[1]
    return pl.pallas_call(
        _bias_kernel,
        grid=(n_heads,),
        in_specs=[pl.BlockSpec(memory_space=pltpu.SMEM)],
        out_specs=pl.BlockSpec((1, N_NEAR_BLOCKS + 1, MOBA_BLOCK, MOBA_BLOCK), lambda h: (h, 0, 0, 0)),
        out_shape=jax.ShapeDtypeStruct((n_heads, N_NEAR_BLOCKS + 1, MOBA_BLOCK, MOBA_BLOCK), F32),
        name="bias_tiles",
    )(rel_bias.astype(F32).reshape(-1))


def _cast_weights_once(pairs):
    @pl.when(pl.program_id(0) == 0)
    def _():
        for w_ref, w_scr in pairs:
            for r in range(0, w_ref.shape[0], CAST_ROWS):
                w_scr[r:r + CAST_ROWS, :] = w_ref[r:r + CAST_ROWS, :].astype(BF16)


def _inproj_kernel(x_ref, g_ref, w_ref, cos_ref, sin_ref, qg_ref, kg_ref,
                   dec_ref, dq_ref, dk_ref, cd_ref, rg_ref,
                   proj_ref, kmean_ref, ret_ref, vt_ref, w_scr, rin_scr, state_ref, *, seq_tiles):
    _cast_weights_once([(w_ref, w_scr)])
    x = x_ref[...]
    ms = jnp.mean(x * x, axis=-1, keepdims=True)
    h = (x * lax.rsqrt(ms + EPS) * g_ref[...]).astype(BF16)
    cos = cos_ref[...]
    sin = sin_ref[...]
    scale = HEAD_DIM ** -0.5
    rows = x.shape[0]
    heads = GROUP_WIDTH // HEAD_DIM

    def head_norm(y, gain):
        return y * lax.rsqrt(jnp.mean(y * y, axis=-1, keepdims=True) + EPS) * gain

    def section(s):
        return jnp.dot(h, w_scr[:, s * GROUP_WIDTH:(s + 1) * GROUP_WIDTH], preferred_element_type=F32)

    for s in range(RET_SECTIONS):
        y = section(s)
        for hh in range(heads):
            yh = y[:, hh * HEAD_DIM:(hh + 1) * HEAD_DIM]
            if s in (0, 1):
                yh = yh * cos + pltpu.roll(yh, HEAD_DIM // 2, axis=1) * sin
                if s == 1:
                    yh = yh * scale
            col = s * GROUP_WIDTH + hh * HEAD_DIM
            rin_scr[:, col:col + HEAD_DIM] = yh.astype(BF16)

    @pl.when(pl.program_id(0) % seq_tiles == 0)
    def _():
        state_ref[...] = jnp.zeros_like(state_ref)

    def moba_section(s):
        y = section(s)
        for hh in range(heads):
            yh = y[:, hh * HEAD_DIM:(hh + 1) * HEAD_DIM]
            if s == 4:
                yh = head_norm(yh, qg_ref[...]) * (scale * LOG2E)
            elif s == 5:
                yh = head_norm(yh, kg_ref[...])
                for blk in range(rows // MOBA_BLOCK):
                    kmean_ref[blk, :, hh * HEAD_DIM:(hh + 1) * HEAD_DIM] = jnp.mean(
                        yh[blk * MOBA_BLOCK:(blk + 1) * MOBA_BLOCK], axis=0, keepdims=True)
            if s == MOBA_V_SECTION:
                vt_ref[hh * HEAD_DIM:(hh + 1) * HEAD_DIM, :] = yh.T.astype(BF16)
            else:
                col = MOBA_PROJ_SECTIONS.index(s) * GROUP_WIDTH + hh * HEAD_DIM
                proj_ref[:, col:col + HEAD_DIM] = yh.astype(BF16)

    chunk = RET_CHUNK
    units = [(c, hh) for c in range(rows // chunk) for hh in range(RET_HEADS)]

    def operand(c, hh, sec):
        return rin_scr[c * chunk:(c + 1) * chunk,
                       sec * GROUP_WIDTH + hh * HEAD_DIM:sec * GROUP_WIDTH + (hh + 1) * HEAD_DIM]

    masked, kv = {}, {}
    for c, hh in units:
        q, k, v = operand(c, hh, 0), operand(c, hh, 1), operand(c, hh, 2)
        sc = lax.dot_general(q, k, _NT, preferred_element_type=F32) * dec_ref[hh]
        masked[c, hh] = sc.astype(BF16)
        v_dec = (v.astype(F32) * dk_ref[hh]).astype(BF16)
        kv[c, hh] = lax.dot_general(k, v_dec, _TN, preferred_element_type=F32)
    before = {}
    for hh in range(RET_HEADS):
        state = state_ref[hh]
        for c in range(rows // chunk):
            before[c, hh] = state.astype(BF16)
            state = state * cd_ref[hh] + kv[c, hh]
        state_ref[hh] = state
    for c, hh in units:
        q, v = operand(c, hh, 0), operand(c, hh, 2)
        q_dec = (q.astype(F32) * dq_ref[hh]).astype(BF16)
        o = jnp.dot(jnp.concatenate([masked[c, hh], q_dec], axis=1),
                    jnp.concatenate([v, before[c, hh]], axis=0), preferred_element_type=F32)
        mu = jnp.mean(o, axis=-1, keepdims=True)
        oc = o - mu
        var = jnp.mean(oc * oc, axis=-1, keepdims=True)
        on = oc * lax.rsqrt(var + EPS) * rg_ref[:, hh * HEAD_DIM:(hh + 1) * HEAD_DIM]
        z = operand(c, hh, 3).astype(F32)
        ret_ref[c * chunk:(c + 1) * chunk, hh * HEAD_DIM:(hh + 1) * HEAD_DIM] = (
            z * jax.nn.sigmoid(z) * on).astype(BF16)

    for s in range(RET_SECTIONS, N_SECTIONS):
        moba_section(s)


def _retention_tables():
    c = RET_CHUNK
    log_g = np.log1p(-np.power(2.0, -5.0 - np.arange(RET_HEADS, dtype=np.float64)))
    j = np.arange(c, dtype=np.float64)
    rel = j[:, None] - j[None, :]
    decay = np.where(rel >= 0, np.exp(log_g[:, None, None] * np.maximum(rel, 0.0)[None]), 0.0)
    dq = np.exp(log_g[:, None] * (j + 1.0)[None])
    dk = np.exp(log_g[:, None] * (c - 1 - j)[None])
    cd = np.exp(log_g * c)
    wide = lambda t: np.broadcast_to(t[:, :, None], (RET_HEADS, c, HEAD_DIM))
    return (jnp.asarray(decay, F32), jnp.asarray(wide(dq), F32), jnp.asarray(wide(dk), F32),
            jnp.asarray(np.broadcast_to(cd[:, None, None], (RET_HEADS, 1, HEAD_DIM)), F32))


def _inproj(x2, layer, norm_g, w_in, cos, sin, q_g, k_g, ret_g, seq):
    m, d_model = x2.shape
    width = w_in.shape[2]
    moba_width = len(MOBA_PROJ_SECTIONS) * GROUP_WIDTH
    tm = ROW_TILE
    seq_tiles = seq // tm
    tables = _retention_tables()
    full = lambda a: pl.BlockSpec(a.shape, lambda i: (0,) * a.ndim)
    row1 = lambda n: pl.BlockSpec((None, 1, n), lambda i: (layer, 0, 0))
    return pl.pallas_call(
        functools.partial(_inproj_kernel, seq_tiles=seq_tiles),
        grid=(m // tm,),
        in_specs=[
            pl.BlockSpec((tm, d_model), lambda i: (i, 0)),
            row1(d_model),
            pl.BlockSpec((None, d_model, width), lambda i: (layer, 0, 0),
                         pipeline_mode=pl.Buffered(1)),
            pl.BlockSpec((tm, HEAD_DIM), lambda i: (i % seq_tiles, 0)),
            pl.BlockSpec((tm, HEAD_DIM), lambda i: (i % seq_tiles, 0)),
            row1(HEAD_DIM), row1(HEAD_DIM),
            *[full(t) for t in tables],
            row1(GROUP_WIDTH),
        ],
        out_specs=[
            pl.BlockSpec((tm, moba_width), lambda i: (i, 0)),
            pl.BlockSpec((tm // MOBA_BLOCK, 1, GROUP_WIDTH), lambda i: (i, 0, 0)),
            pl.BlockSpec((tm, GROUP_WIDTH), lambda i: (i, 0)),
            pl.BlockSpec((None, GROUP_WIDTH, tm), lambda i: (i, 0, 0)),
        ],
        out_shape=[
            jax.ShapeDtypeStruct((m, moba_width), BF16),
            jax.ShapeDtypeStruct((m // MOBA_BLOCK, 1, GROUP_WIDTH), F32),
            jax.ShapeDtypeStruct((m, GROUP_WIDTH), BF16),
            jax.ShapeDtypeStruct((m // tm, GROUP_WIDTH, tm), BF16),
        ],
        scratch_shapes=[pltpu.VMEM((d_model, width), BF16),
                        pltpu.VMEM((tm, RET_SECTIONS * GROUP_WIDTH), BF16),
                        pltpu.VMEM((RET_HEADS, HEAD_DIM, HEAD_DIM), F32)],
        compiler_params=pltpu.CompilerParams(
            dimension_semantics=("arbitrary",), vmem_limit_bytes=VMEM_LIMIT),
        name="in_projection",
    )(x2, norm_g, w_in, cos, sin, q_g, k_g, *tables, ret_g)


def _moba_kernel(rb_ref, q_ref, k_ref, vt_ref, z_ref, km_ref, oh_ref, bias_ref, o_ref,
                 qaug_scr, m_scr, l_scr, acc_scr, s_scr):
    blk = MOBA_BLOCK
    tile = MOBA_TILE
    sub = SUBLANES
    ti = pl.program_id(1)
    n_heads = MOBA_HEADS
    blk_id = lax.broadcasted_iota(jnp.int32, (SEL_LANES, tile), 0)
    blk_f = blk_id.astype(F32)
    q_blk = 2 * ti + lax.broadcasted_iota(jnp.int32, (SEL_LANES, tile), 1) // blk
    past = blk_id < q_blk
    far = (q_blk - blk_id) >= N_NEAR_BLOCKS

    for h in range(n_heads):
        cols = slice(h * HEAD_DIM, (h + 1) * HEAD_DIM)
        q = q_ref[:, cols]
        km = km_ref[0, :, cols]
        km_hi = km.astype(BF16)
        km_lo = (km - km_hi.astype(F32)).astype(BF16)
        gate = lax.dot_general(jnp.concatenate([km_hi, km_lo], axis=1),
                               jnp.concatenate([q, q], axis=1), _NT,
                               preferred_element_type=F32)
        g = jnp.where(past, gate, MASKED_GATE)
        sel = jnp.zeros(gate.shape, F32)
        for _ in range(MOBA_TOPK):
            top = jnp.max(g, axis=0, keepdims=True)
            first = jnp.min(jnp.where(g == top, blk_f, 1e9), axis=0, keepdims=True)
            pick = blk_f == first
            sel = jnp.where(pick, 1.0, sel)
            g = jnp.where(pick, MASKED_GATE, g)
        chosen = jnp.where(past, sel, 0.0) > 0.0
        far_c = jnp.full(gate.shape, rb_ref[(N_BUCKETS - 1) * n_heads + h] * LOG2E, F32)
        far_hi = far_c.astype(BF16).astype(F32)
        mask_hi = jnp.where(chosen, jnp.where(far, far_hi, 0.0), NEG_INF)
        mask_hi = jnp.where(blk_id == q_blk, 0.0, mask_hi)
        mask_lo = jnp.where(chosen, jnp.where(far, far_c - far_hi, 0.0), 0.0)
        q_t = q.astype(F32).T.astype(BF16)
        qaug_scr[h] = jnp.concatenate([q_t, mask_hi.astype(BF16), mask_lo.astype(BF16)], axis=0)

    m_scr[...] = jnp.full(m_scr.shape, NEG_INF, F32)
    l_scr[...] = jnp.zeros(l_scr.shape, F32)
    acc_scr[...] = jnp.zeros(acc_scr.shape, F32)

    def key_rows(step):
        return pl.ds(pl.multiple_of((ti - step) * tile, tile), tile)

    def scores(step, slot, h):
        rows = key_rows(step)
        k_aug = jnp.concatenate([k_ref[rows, h * HEAD_DIM:(h + 1) * HEAD_DIM], oh_ref[rows, :]], axis=1)
        s_scr[slot, h] = jnp.dot(k_aug, qaug_scr[h], preferred_element_type=F32)

    def bias_tile(h, step):
        def part(offset):
            idx = jnp.where(offset < 0, N_NEAR_BLOCKS, jnp.minimum(offset, N_NEAR_BLOCKS))
            return bias_ref[h, idx]
        diag = part(2 * step)
        return jnp.concatenate([jnp.concatenate([diag, part(2 * step + 1)], axis=1),
                                jnp.concatenate([part(2 * step - 1), diag], axis=1)], axis=0)

    def accumulate(step, slot, h, with_bias):
        s = s_scr[slot, h]
        if with_bias:
            s = s + bias_tile(h, step)
        s = s.reshape(tile // sub, sub, tile)
        m_prev = m_scr[h]
        m_tile = jnp.max(jnp.max(s, axis=0), axis=0, keepdims=True)
        m_next = jnp.maximum(m_prev, jnp.broadcast_to(m_tile, m_prev.shape))
        alpha = jnp.exp2(m_prev - m_next)
        p = jnp.exp2(s - m_next[None])
        l_scr[h] = alpha * l_scr[h] + jnp.sum(p, axis=0)
        m_scr[h] = m_next
        pv = jnp.dot(vt_ref[ti - step, h * HEAD_DIM:(h + 1) * HEAD_DIM, :],
                     p.reshape(tile, tile).astype(BF16), preferred_element_type=F32)
        acc = acc_scr[h].reshape(HEAD_DIM // sub, sub, tile) * alpha[None]
        acc_scr[h] = acc.reshape(HEAD_DIM, tile) + pv

    def loop_body(with_bias):
        def one_step(step, slot):
            for h in range(n_heads):
                scores(step + 1, 1 - slot, h)
                accumulate(step, slot, h, with_bias)

        def body(step, carry):
            for slot in (0, 1):
                pl.when((step & 1) == slot)(functools.partial(one_step, step, slot))
            return carry
        return body

    def last_step(with_bias):
        for h in range(n_heads):
            accumulate(ti, ti & 1, h, with_bias)

    n_near = jnp.minimum(ti, MOBA_NEAR_STEPS)
    for h in range(n_heads):
        scores(0, 0, h)
    lax.fori_loop(0, n_near, loop_body(True), 0)
    lax.fori_loop(n_near, ti, loop_body(False), 0)
    pl.when(ti < MOBA_NEAR_STEPS)(functools.partial(last_step, True))
    pl.when(ti >= MOBA_NEAR_STEPS)(functools.partial(last_step, False))
    for h in range(n_heads):
        cols = slice(h * HEAD_DIM, (h + 1) * HEAD_DIM)
        z = z_ref[:, cols].astype(F32)
        denom = jnp.sum(l_scr[h], axis=0, keepdims=True)
        out = (acc_scr[h] / denom).T
        o_ref[:, cols] = (z * jax.nn.sigmoid(z) * out).astype(BF16)


def _moba(proj, v_t, kmean, bias_tiles, rel_bias, batch, seq):
    m = proj.shape[0]
    tile = MOBA_TILE
    nq = seq // tile
    row_blk = np.arange(seq)[:, None] // MOBA_BLOCK
    onehot = jnp.asarray((np.arange(HEAD_DIM)[None, :] & (SEL_LANES - 1)) == row_blk, BF16)
    rows = lambda sec: pl.BlockSpec((tile, GROUP_WIDTH), lambda b, i: (b * nq + i, sec))
    whole = lambda sec: pl.BlockSpec((seq, GROUP_WIDTH), lambda b, i: (b, sec),
                                     pipeline_mode=pl.Buffered(1))
    return pl.pallas_call(
        _moba_kernel,
        grid=(batch, nq),
        in_specs=[pl.BlockSpec(memory_space=pltpu.SMEM),
                  rows(0), whole(1),
                  pl.BlockSpec((nq, GROUP_WIDTH, tile), lambda b, i: (b, 0, 0),
                               pipeline_mode=pl.Buffered(1)),
                  rows(2),
                  pl.BlockSpec((1, SEL_LANES, GROUP_WIDTH), lambda b, i: (b, 0, 0)),
                  pl.BlockSpec(onehot.shape, lambda b, i: (0, 0), pipeline_mode=pl.Buffered(1)),
                  pl.BlockSpec(bias_tiles.shape, lambda b, i: (0, 0, 0, 0),
                               pipeline_mode=pl.Buffered(1))],
        out_specs=pl.BlockSpec((tile, GROUP_WIDTH), lambda b, i: (b * nq + i, 0)),
        out_shape=jax.ShapeDtypeStruct((m, GROUP_WIDTH), BF16),
        scratch_shapes=[pltpu.VMEM((MOBA_HEADS, 2 * HEAD_DIM, tile), BF16),
                        pltpu.VMEM((MOBA_HEADS, SUBLANES, tile), F32),
                        pltpu.VMEM((MOBA_HEADS, SUBLANES, tile), F32),
                        pltpu.VMEM((MOBA_HEADS, HEAD_DIM, tile), F32),
                        pltpu.VMEM((2, MOBA_HEADS, tile, tile), F32)],
        compiler_params=pltpu.CompilerParams(
            dimension_semantics=("arbitrary", "arbitrary"), vmem_limit_bytes=VMEM_LIMIT),
        name="moba",
    )(rel_bias.astype(F32).reshape(-1), proj, proj, v_t, proj, kmean, onehot, bias_tiles)


def _outproj_kernel(x_ref, r_ref, a_ref, wo_ref, p_ref, wp_ref, pg_ref, wg_ref, o_ref,
                    wo_scr, wp_scr, wg_scr):
    _cast_weights_once([(wo_ref, wo_scr), (wp_ref, wp_scr), (wg_ref, wg_scr)])

    def unit_rms(t):
        return t * lax.rsqrt(jnp.mean(t * t, axis=-1, keepdims=True) + EPS)

    half = x_ref.shape[0] // 2
    halves = [slice(0, half), slice(half, 2 * half)]
    for rows in halves:
        y = (jnp.dot(r_ref[rows, :], wo_scr[:GROUP_WIDTH, :], preferred_element_type=F32)
             + jnp.dot(a_ref[rows, :], wo_scr[GROUP_WIDTH:, :], preferred_element_type=F32))
        o_ref[rows, :] = x_ref[rows, :] + y
    for rows in halves:
        e = unit_rms(jnp.dot(p_ref[rows, :].astype(BF16), wp_scr[...], preferred_element_type=F32))
        x1 = o_ref[rows, :]
        gate = jax.nn.sigmoid(jnp.dot(unit_rms(x1).astype(BF16), wg_scr[...],
                                      preferred_element_type=F32))
        o_ref[rows, :] = x1 + gate * (e * pg_ref[...])


def _outproj(x2, r_out, m_out, layer, w_out, p3, w_ple, ple_g, w_gate):
    m, d_model = x2.shape
    tm = OUT_ROW_TILE
    rows = lambda a: pl.BlockSpec((tm, a.shape[1]), lambda i: (i, 0))
    full = lambda a: pl.BlockSpec((None,) + a.shape[1:], lambda i: (layer, 0, 0),
                                  pipeline_mode=pl.Buffered(1))
    return pl.pallas_call(
        _outproj_kernel,
        grid=(m // tm,),
        in_specs=[rows(x2), rows(r_out), rows(m_out), full(w_out),
                  pl.BlockSpec((None, tm, p3.shape[2]), lambda i: (layer, i, 0)), full(w_ple),
                  full(ple_g), full(w_gate)],
        out_specs=pl.BlockSpec((tm, d_model), lambda i: (i, 0)),
        out_shape=jax.ShapeDtypeStruct((m, d_model), F32),
        scratch_shapes=[pltpu.VMEM(w.shape[1:], BF16) for w in (w_out, w_ple, w_gate)],
        compiler_params=pltpu.CompilerParams(
            dimension_semantics=("arbitrary",), vmem_limit_bytes=VMEM_LIMIT),
        name="out_projection",
    )(x2, r_out, m_out, w_out, p3, w_ple, ple_g, w_gate)


def _rotary_tables(seq):
    half = HEAD_DIM // 2
    inv = np.float32(ROPE_THETA) ** (-np.arange(half, dtype=np.float32) / np.float32(half))
    ang = np.arange(seq, dtype=np.float32)[:, None] * inv[None, :]
    cos, sin = np.cos(ang), np.sin(ang)
    return (jnp.asarray(np.concatenate([cos, cos], axis=1), F32),
            jnp.asarray(np.concatenate([-sin, sin], axis=1), F32))


def kernel(x, p, norm_g, w_in, ret_norm_g, q_norm_g, k_norm_g, rel_bias, w_out, w_ple, ple_norm_g, w_ple_gate):
    batch, seq, d_model = x.shape
    depth = w_in.shape[0]
    m = batch * seq
    assert w_in.shape[2] == N_SECTIONS * GROUP_WIDTH and w_out.shape[1] == 2 * GROUP_WIDTH
    assert rel_bias.shape == (N_BUCKETS, MOBA_HEADS)
    assert seq % ROW_TILE == 0 and ROW_TILE % MOBA_BLOCK == 0 and seq % MOBA_TILE == 0
    assert m % OUT_ROW_TILE == 0
    assert ROW_TILE == MOBA_TILE
    assert ROW_TILE % RET_CHUNK == 0 and seq // MOBA_BLOCK <= SEL_LANES

    cos, sin = _rotary_tables(seq)
    bias_tiles = _bias_tiles(rel_bias)
    x2 = x.reshape(m, d_model)
    nb = seq // MOBA_BLOCK
    stack = lambda g: g.reshape(depth, 1, -1)
    p3 = p.reshape(depth, m, -1)
    for i in range(depth):
        proj, kmean, r_out, v_t = _inproj(x2, i, stack(norm_g), w_in, cos, sin,
                                     stack(q_norm_g), stack(k_norm_g), stack(ret_norm_g), seq)
        kmean = jnp.pad(kmean.reshape(batch, nb, GROUP_WIDTH), ((0, 0), (0, SEL_LANES - nb), (0, 0)))
        m_out = _moba(proj, v_t, kmean, bias_tiles, rel_bias, batch, seq)
        x2 = _outproj(x2, r_out, m_out, i, w_out, p3, w_ple, stack(ple_norm_g), w_ple_gate)
    return x2.reshape(batch, seq, d_model)
```

```python
import functools
import math

import numpy as np
import jax
import jax.numpy as jnp
from jax import lax
from jax.experimental import pallas as pl
from jax.experimental.pallas import tpu as pltpu

F32 = jnp.float32
BF16 = jnp.bfloat16

HEAD_DIM = 128
SUBLANES = 8
BF16_ROWS = 16
RET_HEADS = 4
MOBA_HEADS = 4
GROUP_WIDTH = 512
N_SECTIONS = 8
RET_SECTIONS = 4
MOBA_V_SECTION = 6
MOBA_PROJ_SECTIONS = (4, 5, 7)
RET_CHUNK = 128
MOBA_BLOCK = 256
MOBA_TOPK = 3
N_BUCKETS = 32
MAX_DISTANCE = 2048
ROPE_THETA = 10000.0
EPS = 1e-6
NEG_INF = -1e30
MASKED_GATE = -3e38
LOG2E = math.log2(math.e)
SEL_LANES = HEAD_DIM // 2

ROW_TILE = 512
OUT_ROW_TILE = 1024
CAST_ROWS = 256
VMEM_LIMIT = 56 * 1024 * 1024

_NT = (((1,), (1,)), ((), ()))
_TN = (((0,), (0,)), ((), ()))


def _bucket_thresholds():
    max_exact = N_BUCKETS // 2
    thr = []
    for k in range(N_BUCKETS - max_exact):
        n = max_exact
        while min(max_exact + int(math.log(n / max_exact) / math.log(MAX_DISTANCE / max_exact)
                                  * (N_BUCKETS - max_exact)), N_BUCKETS - 1) < max_exact + k:
            n += 1
        thr.append(n)
    return thr


_BUCKET_THR = _bucket_thresholds()
N_NEAR = next(d for d in range(1, 1 << 20) if (d - 1) * MOBA_BLOCK + 1 >= _BUCKET_THR[-1])
N_NEAR_BLOCKS = N_NEAR
MOBA_TILE = 2 * MOBA_BLOCK
MOBA_NEAR_STEPS = N_NEAR_BLOCKS // 2 + 1


def _bias_kernel(rb_ref, out_ref):
    h = pl.program_id(0)
    n_heads = pl.num_programs(0)
    c = lax.broadcasted_iota(jnp.int32, (MOBA_BLOCK, MOBA_BLOCK), 0)
    r = lax.broadcasted_iota(jnp.int32, (MOBA_BLOCK, MOBA_BLOCK), 1)
    max_exact = N_BUCKETS // 2

    def rb(b):
        return rb_ref[b * n_heads + h] * LOG2E

    for d in range(N_NEAR_BLOCKS):
        n = d * MOBA_BLOCK + r - c
        nn = jnp.maximum(n, 0)
        val = jnp.full((MOBA_BLOCK, MOBA_BLOCK), rb(0), F32)
        for b in range(1, max_exact):
            val = jnp.where(nn >= b, rb(b), val)
        for k, thr in enumerate(_BUCKET_THR):
            val = jnp.where(nn >= thr, rb(max_exact + k), val)
        if d == 0:
            val = jnp.where(n >= 0, val, NEG_INF)
        out_ref[0, d] = val
    out_ref[0, N_NEAR_BLOCKS] = jnp.zeros((MOBA_BLOCK, MOBA_BLOCK), F32)


def _bias_tiles(rel_bias):
    n_heads = rel_bias.shape[1]
    return pl.pallas_call(
        _bias_kernel,
        grid=(n_heads,),
        in_specs=[pl.BlockSpec(memory_space=pltpu.SMEM)],
        out_specs=pl.BlockSpec((1, N_NEAR_BLOCKS + 1, MOBA_BLOCK, MOBA_BLOCK), lambda h: (h, 0, 0, 0)),
        out_shape=jax.ShapeDtypeStruct((n_heads, N_NEAR_BLOCKS + 1, MOBA_BLOCK, MOBA_BLOCK), F32),
        name="bias_tiles",
    )(rel_bias.astype(F32).reshape(-1))


def _cast_weights_once(pairs):
    @pl.when(pl.program_id(0) == 0)
    def _():
        for w_ref, w_scr in pairs:
            for r in range(0, w_ref.shape[0], CAST_ROWS):
                w_scr[r:r + CAST_ROWS, :] = w_ref[r:r + CAST_ROWS, :].astype(BF16)


def _inproj_kernel(x_ref, g_ref, w_ref, cos_ref, sin_ref, qg_ref, kg_ref,
                   dec_ref, dq_ref, dk_ref, cd_ref, rg_ref,
                   proj_ref, kmean_ref, ret_ref, vt_ref, w_scr, rin_scr, state_ref, *, seq_tiles):
    _cast_weights_once([(w_ref, w_scr)])
    x = x_ref[...]
    ms = jnp.mean(x * x, axis=-1, keepdims=True)
    h = (x * lax.rsqrt(ms + EPS) * g_ref[...]).astype(BF16)
    cos = cos_ref[...]
    sin = sin_ref[...]
    scale = HEAD_DIM ** -0.5
    rows = x.shape[0]
    heads = GROUP_WIDTH // HEAD_DIM

    def head_norm(y, gain):
        return y * lax.rsqrt(jnp.mean(y * y, axis=-1, keepdims=True) + EPS) * gain

    def section(s):
        return jnp.dot(h, w_scr[:, s * GROUP_WIDTH:(s + 1) * GROUP_WIDTH], preferred_element_type=F32)

    for s in range(RET_SECTIONS):
        y = section(s)
        for hh in range(heads):
            yh = y[:, hh * HEAD_DIM:(hh + 1) * HEAD_DIM]
            if s in (0, 1):
                yh = yh * cos + pltpu.roll(yh, HEAD_DIM // 2, axis=1) * sin
                if s == 1:
                    yh = yh * scale
            col = s * GROUP_WIDTH + hh * HEAD_DIM
            rin_scr[:, col:col + HEAD_DIM] = yh.astype(BF16)

    @pl.when(pl.program_id(0) % seq_tiles == 0)
    def _():
        state_ref[...] = jnp.zeros_like(state_ref)

    def moba_section(s):
        y = section(s)
        for hh in range(heads):
            yh = y[:, hh * HEAD_DIM:(hh + 1) * HEAD_DIM]
            if s == 4:
                yh = head_norm(yh, qg_ref[...]) * (scale * LOG2E)
            elif s == 5:
                yh = head_norm(yh, kg_ref[...])
                for blk in range(rows // MOBA_BLOCK):
                    kmean_ref[blk, :, hh * HEAD_DIM:(hh + 1) * HEAD_DIM] = jnp.mean(
                        yh[blk * MOBA_BLOCK:(blk + 1) * MOBA_BLOCK], axis=0, keepdims=True)
            if s == MOBA_V_SECTION:
                vt_ref[hh * HEAD_DIM:(hh + 1) * HEAD_DIM, :] = yh.T.astype(BF16)
            else:
                col = MOBA_PROJ_SECTIONS.index(s) * GROUP_WIDTH + hh * HEAD_DIM
                proj_ref[:, col:col + HEAD_DIM] = yh.astype(BF16)

    chunk = RET_CHUNK
    units = [(c, hh) for c in range(rows // chunk) for hh in range(RET_HEADS)]

    def operand(c, hh, sec):
        return rin_scr[c * chunk:(c + 1) * chunk,
                       sec * GROUP_WIDTH + hh * HEAD_DIM:sec * GROUP_WIDTH + (hh + 1) * HEAD_DIM]

    masked, kv = {}, {}
    for c, hh in units:
        q, k, v = operand(c, hh, 0), operand(c, hh, 1), operand(c, hh, 2)
        sc = lax.dot_general(q, k, _NT, preferred_element_type=F32) * dec_ref[hh]
        masked[c, hh] = sc.astype(BF16)
        v_dec = (v.astype(F32) * dk_ref[hh]).astype(BF16)
        kv[c, hh] = lax.dot_general(k, v_dec, _TN, preferred_element_type=F32)
    before = {}
    for hh in range(RET_HEADS):
        state = state_ref[hh]
        for c in range(rows // chunk):
            before[c, hh] = state.astype(BF16)
            state = state * cd_ref[hh] + kv[c, hh]
        state_ref[hh] = state
    for c, hh in units:
        q, v = operand(c, hh, 0), operand(c, hh, 2)
        q_dec = (q.astype(F32) * dq_ref[hh]).astype(BF16)
        o = jnp.dot(jnp.concatenate([masked[c, hh], q_dec], axis=1),
                    jnp.concatenate([v, before[c, hh]], axis=0), preferred_element_type=F32)
        mu = jnp.mean(o, axis=-1, keepdims=True)
        oc = o - mu
        var = jnp.mean(oc * oc, axis=-1, keepdims=True)
        on = oc * lax.rsqrt(var + EPS) * rg_ref[:, hh * HEAD_DIM:(hh + 1) * HEAD_DIM]
        z = operand(c, hh, 3).astype(F32)
        ret_ref[c * chunk:(c + 1) * chunk, hh * HEAD_DIM:(hh + 1) * HEAD_DIM] = (
            z * jax.nn.sigmoid(z) * on).astype(BF16)

    for s in range(RET_SECTIONS, N_SECTIONS):
        moba_section(s)


def _retention_tables():
    c = RET_CHUNK
    log_g = np.log1p(-np.power(2.0, -5.0 - np.arange(RET_HEADS, dtype=np.float64)))
    j = np.arange(c, dtype=np.float64)
    rel = j[:, None] - j[None, :]
    decay = np.where(rel >= 0, np.exp(log_g[:, None, None] * np.maximum(rel, 0.0)[None]), 0.0)
    dq = np.exp(log_g[:, None] * (j + 1.0)[None])
    dk = np.exp(log_g[:, None] * (c - 1 - j)[None])
    cd = np.exp(log_g * c)
    wide = lambda t: np.broadcast_to(t[:, :, None], (RET_HEADS, c, HEAD_DIM))
    return (jnp.asarray(decay, F32), jnp.asarray(wide(dq), F32), jnp.asarray(wide(dk), F32),
            jnp.asarray(np.broadcast_to(cd[:, None, None], (RET_HEADS, 1, HEAD_DIM)), F32))


def _inproj(x2, layer, norm_g, w_in, cos, sin, q_g, k_g, ret_g, seq):
    m, d_model = x2.shape
    width = w_in.shape[2]
    moba_width = len(MOBA_PROJ_SECTIONS) * GROUP_WIDTH
    tm = ROW_TILE
    seq_tiles = seq // tm
    tables = _retention_tables()
    full = lambda a: pl.BlockSpec(a.shape, lambda i: (0,) * a.ndim)
    row1 = lambda n: pl.BlockSpec((None, 1, n), lambda i: (layer, 0, 0))
    return pl.pallas_call(
        functools.partial(_inproj_kernel, seq_tiles=seq_tiles),
        grid=(m // tm,),
        in_specs=[
            pl.BlockSpec((tm, d_model), lambda i: (i, 0)),
            row1(d_model),
            pl.BlockSpec((None, d_model, width), lambda i: (layer, 0, 0),
                         pipeline_mode=pl.Buffered(1)),
            pl.BlockSpec((tm, HEAD_DIM), lambda i: (i % seq_tiles, 0)),
            pl.BlockSpec((tm, HEAD_DIM), lambda i: (i % seq_tiles, 0)),
            row1(HEAD_DIM), row1(HEAD_DIM),
            *[full(t) for t in tables],
            row1(GROUP_WIDTH),
        ],
        out_specs=[
            pl.BlockSpec((tm, moba_width), lambda i: (i, 0)),
            pl.BlockSpec((tm // MOBA_BLOCK, 1, GROUP_WIDTH), lambda i: (i, 0, 0)),
            pl.BlockSpec((tm, GROUP_WIDTH), lambda i: (i, 0)),
            pl.BlockSpec((None, GROUP_WIDTH, tm), lambda i: (i, 0, 0)),
        ],
        out_shape=[
            jax.ShapeDtypeStruct((m, moba_width), BF16),
            jax.ShapeDtypeStruct((m // MOBA_BLOCK, 1, GROUP_WIDTH), F32),
            jax.ShapeDtypeStruct((m, GROUP_WIDTH), BF16),
            jax.ShapeDtypeStruct((m // tm, GROUP_WIDTH, tm), BF16),
        ],
        scratch_shapes=[pltpu.VMEM((d_model, width), BF16),
                        pltpu.VMEM((tm, RET_SECTIONS * GROUP_WIDTH), BF16),
                        pltpu.VMEM((RET_HEADS, HEAD_DIM, HEAD_DIM), F32)],
        compiler_params=pltpu.CompilerParams(
            dimension_semantics=("arbitrary",), vmem_limit_bytes=VMEM_LIMIT),
        name="in_projection",
    )(x2, norm_g, w_in, cos, sin, q_g, k_g, *tables, ret_g)


def _moba_kernel(rb_ref, q_ref, k_ref, vt_ref, z_ref, km_ref, oh_ref, bias_ref, o_ref,
                 qaug_scr, m_scr, l_scr, acc_scr, s_scr):
    blk = MOBA_BLOCK
    tile = MOBA_TILE
    sub = SUBLANES
    ti = pl.program_id(1)
    n_heads = MOBA_HEADS
    sel_rows = -(-(k_ref.shape[0] // blk) // BF16_ROWS) * BF16_ROWS
    blk_id = lax.broadcasted_iota(jnp.int32, (sel_rows, tile), 0)
    blk_f = blk_id.astype(F32)
    q_blk = 2 * ti + lax.broadcasted_iota(jnp.int32, (sel_rows, tile), 1) // blk
    unused = jnp.zeros((SEL_LANES - sel_rows, tile), BF16)
    past = blk_id < q_blk
    far = (q_blk - blk_id) >= N_NEAR_BLOCKS

    for h in range(n_heads):
        cols = slice(h * HEAD_DIM, (h + 1) * HEAD_DIM)
        q = q_ref[:, cols]
        km = km_ref[0, :sel_rows, cols]
        km_hi = km.astype(BF16)
        km_lo = (km - km_hi.astype(F32)).astype(BF16)
        gate = lax.dot_general(jnp.concatenate([km_hi, km_lo], axis=1),
                               jnp.concatenate([q, q], axis=1), _NT,
                               preferred_element_type=F32)
        g = jnp.where(past, gate, MASKED_GATE)
        sel = jnp.zeros(gate.shape, F32)
        for _ in range(MOBA_TOPK):
            top = jnp.max(g, axis=0, keepdims=True)
            first = jnp.min(jnp.where(g == top, blk_f, 1e9), axis=0, keepdims=True)
            pick = blk_f == first
            sel = jnp.where(pick, 1.0, sel)
            g = jnp.where(pick, MASKED_GATE, g)
        chosen = jnp.where(past, sel, 0.0) > 0.0
        far_c = jnp.full(gate.shape, rb_ref[(N_BUCKETS - 1) * n_heads + h] * LOG2E, F32)
        far_hi = far_c.astype(BF16).astype(F32)
        mask_hi = jnp.where(chosen, jnp.where(far, far_hi, 0.0), NEG_INF)
        mask_hi = jnp.where(blk_id == q_blk, 0.0, mask_hi)
        mask_lo = jnp.where(chosen, jnp.where(far, far_c - far_hi, 0.0), 0.0)
        q_t = q.astype(F32).T.astype(BF16)
        qaug_scr[h] = jnp.concatenate(
            [q_t, mask_hi.astype(BF16), unused, mask_lo.astype(BF16), unused], axis=0)

    m_scr[...] = jnp.full(m_scr.shape, NEG_INF, F32)
    l_scr[...] = jnp.zeros(l_scr.shape, F32)
    acc_scr[...] = jnp.zeros(acc_scr.shape, F32)

    def key_rows(step):
        return pl.ds(pl.multiple_of((ti - step) * tile, tile), tile)

    def bias_tile(h, step):
        def part(offset):
            idx = jnp.where(offset < 0, N_NEAR_BLOCKS, jnp.minimum(offset, N_NEAR_BLOCKS))
            return bias_ref[h, idx]
        diag = part(2 * step)
        return jnp.concatenate([jnp.concatenate([diag, part(2 * step + 1)], axis=1),
                                jnp.concatenate([part(2 * step - 1), diag], axis=1)], axis=0)

    def scores(step, slot, h, with_bias):
        rows = key_rows(step)
        k_aug = jnp.concatenate([k_ref[rows, h * HEAD_DIM:(h + 1) * HEAD_DIM], oh_ref[rows, :]], axis=1)
        s = jnp.dot(k_aug, qaug_scr[h], preferred_element_type=F32)
        if with_bias:
            s = s + bias_tile(h, step)
        s_scr[slot, h] = s

    def accumulate(step, slot, h):
        s = s_scr[slot, h].reshape(tile // sub, sub, tile)
        m_prev = m_scr[h]
        m_tile = jnp.max(jnp.max(s, axis=0), axis=0, keepdims=True)
        m_next = jnp.maximum(m_prev, jnp.broadcast_to(m_tile, m_prev.shape))
        alpha = jnp.exp2(m_prev - m_next)
        p = jnp.exp2(s - m_next[None])
        l_scr[h] = alpha * l_scr[h] + jnp.sum(p, axis=0)
        m_scr[h] = m_next
        pv = jnp.dot(vt_ref[ti - step, h * HEAD_DIM:(h + 1) * HEAD_DIM, :],
                     p.reshape(tile, tile).astype(BF16), preferred_element_type=F32)
        acc = acc_scr[h].reshape(HEAD_DIM // sub, sub, tile) * alpha[None]
        acc_scr[h] = acc.reshape(HEAD_DIM, tile) + pv

    def loop_body(next_with_bias):
        def one_step(step, slot):
            for h in range(n_heads):
                scores(step + 1, 1 - slot, h, next_with_bias)
                accumulate(step, slot, h)

        def body(step, carry):
            for slot in (0, 1):
                pl.when((step & 1) == slot)(functools.partial(one_step, step, slot))
            return carry
        return body

    n_biased = jnp.minimum(ti, MOBA_NEAR_STEPS - 1)
    for h in range(n_heads):
        scores(0, 0, h, True)
    lax.fori_loop(0, n_biased, loop_body(True), 0)
    lax.fori_loop(n_biased, ti, loop_body(False), 0)
    for h in range(n_heads):
        accumulate(ti, ti & 1, h)
    for h in range(n_heads):
        cols = slice(h * HEAD_DIM, (h + 1) * HEAD_DIM)
        z = z_ref[:, cols].astype(F32)
        denom = jnp.sum(l_scr[h], axis=0, keepdims=True)
        out = (acc_scr[h] / denom).T
        o_ref[:, cols] = (z * jax.nn.sigmoid(z) * out).astype(BF16)


def _moba(proj, v_t, kmean, bias_tiles, rel_bias, batch, seq):
    m = proj.shape[0]
    tile = MOBA_TILE
    nq = seq // tile
    row_blk = np.arange(seq)[:, None] // MOBA_BLOCK
    onehot = jnp.asarray((np.arange(HEAD_DIM)[None, :] & (SEL_LANES - 1)) == row_blk, BF16)
    rows = lambda sec: pl.BlockSpec((tile, GROUP_WIDTH), lambda b, i: (b * nq + i, sec))
    whole = lambda sec: pl.BlockSpec((seq, GROUP_WIDTH), lambda b, i: (b, sec))
    return pl.pallas_call(
        _moba_kernel,
        grid=(batch, nq),
        in_specs=[pl.BlockSpec(memory_space=pltpu.SMEM),
                  rows(0), whole(1),
                  pl.BlockSpec((nq, GROUP_WIDTH, tile), lambda b, i: (b, 0, 0),
                               pipeline_mode=pl.Buffered(1)),
                  rows(2),
                  pl.BlockSpec((1, SEL_LANES, GROUP_WIDTH), lambda b, i: (b, 0, 0)),
                  pl.BlockSpec(onehot.shape, lambda b, i: (0, 0), pipeline_mode=pl.Buffered(1)),
                  pl.BlockSpec(bias_tiles.shape, lambda b, i: (0, 0, 0, 0),
                               pipeline_mode=pl.Buffered(1))],
        out_specs=pl.BlockSpec((tile, GROUP_WIDTH), lambda b, i: (b * nq + i, 0)),
        out_shape=jax.ShapeDtypeStruct((m, GROUP_WIDTH), BF16),
        scratch_shapes=[pltpu.VMEM((MOBA_HEADS, 2 * HEAD_DIM, tile), BF16),
                        pltpu.VMEM((MOBA_HEADS, SUBLANES, tile), F32),
                        pltpu.VMEM((MOBA_HEADS, SUBLANES, tile), F32),
                        pltpu.VMEM((MOBA_HEADS, HEAD_DIM, tile), F32),
                        pltpu.VMEM((2, MOBA_HEADS, tile, tile), F32)],
        compiler_params=pltpu.CompilerParams(
            dimension_semantics=("arbitrary", "arbitrary"), vmem_limit_bytes=VMEM_LIMIT),
        name="moba",
    )(rel_bias.astype(F32).reshape(-1), proj, proj, v_t, proj, kmean, onehot, bias_tiles)


def _outproj_kernel(x_ref, r_ref, a_ref, wo_ref, p_ref, wp_ref, pg_ref, wg_ref, o_ref,
                    wo_scr, wp_scr, wg_scr):
    _cast_weights_once([(wo_ref, wo_scr), (wp_ref, wp_scr), (wg_ref, wg_scr)])

    def unit_rms(t):
        return t * lax.rsqrt(jnp.mean(t * t, axis=-1, keepdims=True) + EPS)

    half = x_ref.shape[0] // 2
    halves = [slice(0, half), slice(half, 2 * half)]
    for rows in halves:
        y = (jnp.dot(r_ref[rows, :], wo_scr[:GROUP_WIDTH, :], preferred_element_type=F32)
             + jnp.dot(a_ref[rows, :], wo_scr[GROUP_WIDTH:, :], preferred_element_type=F32))
        o_ref[rows, :] = x_ref[rows, :] + y
    for rows in halves:
        e = unit_rms(jnp.dot(p_ref[rows, :].astype(BF16), wp_scr[...], preferred_element_type=F32))
        x1 = o_ref[rows, :]
        gate = jax.nn.sigmoid(jnp.dot(unit_rms(x1).astype(BF16), wg_scr[...],
                                      preferred_element_type=F32))
        o_ref[rows, :] = x1 + gate * (e * pg_ref[...])


def _outproj(x2, r_out, m_out, layer, w_out, p3, w_ple, ple_g, w_gate):
    m, d_model = x2.shape
    tm = OUT_ROW_TILE
    rows = lambda a: pl.BlockSpec((tm, a.shape[1]), lambda i: (i, 0))
    full = lambda a: pl.BlockSpec((None,) + a.shape[1:], lambda i: (layer, 0, 0),
                                  pipeline_mode=pl.Buffered(1))
    return pl.pallas_call(
        _outproj_kernel,
        grid=(m // tm,),
        in_specs=[rows(x2), rows(r_out), rows(m_out), full(w_out),
                  pl.BlockSpec((None, tm, p3.shape[2]), lambda i: (layer, i, 0)), full(w_ple),
                  full(ple_g), full(w_gate)],
        out_specs=pl.BlockSpec((tm, d_model), lambda i: (i, 0)),
        out_shape=jax.ShapeDtypeStruct((m, d_model), F32),
        scratch_shapes=[pltpu.VMEM(w.shape[1:], BF16) for w in (w_out, w_ple, w_gate)],
        compiler_params=pltpu.CompilerParams(
            dimension_semantics=("arbitrary",), vmem_limit_bytes=VMEM_LIMIT),
        name="out_projection",
    )(x2, r_out, m_out, w_out, p3, w_ple, ple_g, w_gate)


def _rotary_tables(seq):
    half = HEAD_DIM // 2
    inv = np.float32(ROPE_THETA) ** (-np.arange(half, dtype=np.float32) / np.float32(half))
    ang = np.arange(seq, dtype=np.float32)[:, None] * inv[None, :]
    cos, sin = np.cos(ang), np.sin(ang)
    return (jnp.asarray(np.concatenate([cos, cos], axis=1), F32),
            jnp.asarray(np.concatenate([-sin, sin], axis=1), F32))


def kernel(x, p, norm_g, w_in, ret_norm_g, q_norm_g, k_norm_g, rel_bias, w_out, w_ple, ple_norm_g, w_ple_gate):
    batch, seq, d_model = x.shape
    depth = w_in.shape[0]
    m = batch * seq
    assert w_in.shape[2] == N_SECTIONS * GROUP_WIDTH and w_out.shape[1] == 2 * GROUP_WIDTH
    assert rel_bias.shape == (N_BUCKETS, MOBA_HEADS)
    assert seq % ROW_TILE == 0 and ROW_TILE % MOBA_BLOCK == 0 and seq % MOBA_TILE == 0
    assert m % OUT_ROW_TILE == 0
    assert ROW_TILE == MOBA_TILE
    assert ROW_TILE % RET_CHUNK == 0 and seq // MOBA_BLOCK <= SEL_LANES

    cos, sin = _rotary_tables(seq)
    bias_tiles = _bias_tiles(rel_bias)
    x2 = x.reshape(m, d_model)
    nb = seq // MOBA_BLOCK
    stack = lambda g: g.reshape(depth, 1, -1)
    p3 = p.reshape(depth, m, -1)
    for i in range(depth):
        proj, kmean, r_out, v_t = _inproj(x2, i, stack(norm_g), w_in, cos, sin,
                                     stack(q_norm_g), stack(k_norm_g), stack(ret_norm_g), seq)
        kmean = jnp.pad(kmean.reshape(batch, nb, GROUP_WIDTH), ((0, 0), (0, SEL_LANES - nb), (0, 0)))
        m_out = _moba(proj, v_t, kmean, bias_tiles, rel_bias, batch, seq)
        x2 = _outproj(x2, r_out, m_out, i, w_out, p3, w_ple, stack(ple_norm_g), w_ple_gate)
    return x2.reshape(batch, seq, d_model)
```

```python
import functools
import math

import numpy as np
import jax
import jax.numpy as jnp
from jax import lax
from jax.experimental import pallas as pl
from jax.experimental.pallas import tpu as pltpu

F32 = jnp.float32
BF16 = jnp.bfloat16

HEAD_DIM = 128
SUBLANES = 8
BF16_ROWS = 16
ACC_ROWS = HEAD_DIM + BF16_ROWS
RET_HEADS = 4
MOBA_HEADS = 4
GROUP_WIDTH = 512
N_SECTIONS = 8
RET_SECTIONS = 4
MOBA_V_SECTION = 6
MOBA_PROJ_SECTIONS = (4, 5, 7)
RET_CHUNK = 128
MOBA_BLOCK = 256
MOBA_TOPK = 3
N_BUCKETS = 32
MAX_DISTANCE = 2048
ROPE_THETA = 10000.0
EPS = 1e-6
NEG_INF = -1e30
MASKED_GATE = -3e38
LOG2E = math.log2(math.e)
SEL_LANES = HEAD_DIM // 2

ROW_TILE = 512
OUT_ROW_TILE = 1024
CAST_ROWS = 256
VMEM_LIMIT = 56 * 1024 * 1024

_NT = (((1,), (1,)), ((), ()))
_TN = (((0,), (0,)), ((), ()))


def _bucket_thresholds():
    max_exact = N_BUCKETS // 2
    thr = []
    for k in range(N_BUCKETS - max_exact):
        n = max_exact
        while min(max_exact + int(math.log(n / max_exact) / math.log(MAX_DISTANCE / max_exact)
                                  * (N_BUCKETS - max_exact)), N_BUCKETS - 1) < max_exact + k:
            n += 1
        thr.append(n)
    return thr


_BUCKET_THR = _bucket_thresholds()
N_NEAR = next(d for d in range(1, 1 << 20) if (d - 1) * MOBA_BLOCK + 1 >= _BUCKET_THR[-1])
N_NEAR_BLOCKS = N_NEAR
MOBA_TILE = 2 * MOBA_BLOCK
MOBA_NEAR_STEPS = N_NEAR_BLOCKS // 2 + 1


def _bias_kernel(rb_ref, out_ref):
    h = pl.program_id(0)
    n_heads = pl.num_programs(0)
    c = lax.broadcasted_iota(jnp.int32, (MOBA_BLOCK, MOBA_BLOCK), 0)
    r = lax.broadcasted_iota(jnp.int32, (MOBA_BLOCK, MOBA_BLOCK), 1)
    max_exact = N_BUCKETS // 2

    def rb(b):
        return rb_ref[b * n_heads + h] * LOG2E

    for d in range(N_NEAR_BLOCKS):
        n = d * MOBA_BLOCK + r - c
        nn = jnp.maximum(n, 0)
        val = jnp.full((MOBA_BLOCK, MOBA_BLOCK), rb(0), F32)
        for b in range(1, max_exact):
            val = jnp.where(nn >= b, rb(b), val)
        for k, thr in enumerate(_BUCKET_THR):
            val = jnp.where(nn >= thr, rb(max_exact + k), val)
        if d == 0:
            val = jnp.where(n >= 0, val, NEG_INF)
        out_ref[0, d] = val
    out_ref[0, N_NEAR_BLOCKS] = jnp.zeros((MOBA_BLOCK, MOBA_BLOCK), F32)


def _bias_tiles(rel_bias):
    n_heads = rel_bias.shape[1]
    return pl.pallas_call(
        _bias_kernel,
        grid=(n_heads,),
        in_specs=[pl.BlockSpec(memory_space=pltpu.SMEM)],
        out_specs=pl.BlockSpec((1, N_NEAR_BLOCKS + 1, MOBA_BLOCK, MOBA_BLOCK), lambda h: (h, 0, 0, 0)),
        out_shape=jax.ShapeDtypeStruct((n_heads, N_NEAR_BLOCKS + 1, MOBA_BLOCK, MOBA_BLOCK), F32),
        name="bias_tiles",
    )(rel_bias.astype(F32).reshape(-1))


def _cast_weights_once(pairs):
    @pl.when(pl.program_id(0) == 0)
    def _():
        for w_ref, w_scr in pairs:
            for r in range(0, w_ref.shape[0], CAST_ROWS):
                w_scr[r:r + CAST_ROWS, :] = w_ref[r:r + CAST_ROWS, :].astype(BF16)


def _inproj_kernel(x_ref, g_ref, w_ref, cos_ref, sin_ref, qg_ref, kg_ref,
                   dec_ref, dq_ref, dk_ref, cd_ref, rg_ref,
                   proj_ref, kmean_ref, ret_ref, vt_ref, w_scr, rin_scr, state_ref, *, seq_tiles):
    _cast_weights_once([(w_ref, w_scr)])
    x = x_ref[...]
    ms = jnp.mean(x * x, axis=-1, keepdims=True)
    h = (x * lax.rsqrt(ms + EPS) * g_ref[...]).astype(BF16)
    cos = cos_ref[...]
    sin = sin_ref[...]
    scale = HEAD_DIM ** -0.5
    rows = x.shape[0]
    heads = GROUP_WIDTH // HEAD_DIM

    def head_norm(y, gain):
        return y * lax.rsqrt(jnp.mean(y * y, axis=-1, keepdims=True) + EPS) * gain

    def section(s):
        return jnp.dot(h, w_scr[:, s * GROUP_WIDTH:(s + 1) * GROUP_WIDTH], preferred_element_type=F32)

    for s in range(RET_SECTIONS):
        y = section(s)
        for hh in range(heads):
            yh = y[:, hh * HEAD_DIM:(hh + 1) * HEAD_DIM]
            if s in (0, 1):
                yh = yh * cos + pltpu.roll(yh, HEAD_DIM // 2, axis=1) * sin
                if s == 1:
                    yh = yh * scale
            col = s * GROUP_WIDTH + hh * HEAD_DIM
            rin_scr[:, col:col + HEAD_DIM] = yh.astype(BF16)

    @pl.when(pl.program_id(0) % seq_tiles == 0)
    def _():
        state_ref[...] = jnp.zeros_like(state_ref)

    def moba_section(s):
        y = section(s)
        for hh in range(heads):
            yh = y[:, hh * HEAD_DIM:(hh + 1) * HEAD_DIM]
            if s == 4:
                yh = head_norm(yh, qg_ref[...]) * (scale * LOG2E)
            elif s == 5:
                yh = head_norm(yh, kg_ref[...])
                for blk in range(rows // MOBA_BLOCK):
                    kmean_ref[blk, :, hh * HEAD_DIM:(hh + 1) * HEAD_DIM] = jnp.mean(
                        yh[blk * MOBA_BLOCK:(blk + 1) * MOBA_BLOCK], axis=0, keepdims=True)
            if s == MOBA_V_SECTION:
                vt_ref[hh * HEAD_DIM:(hh + 1) * HEAD_DIM, :] = yh.T.astype(BF16)
            else:
                col = MOBA_PROJ_SECTIONS.index(s) * GROUP_WIDTH + hh * HEAD_DIM
                proj_ref[:, col:col + HEAD_DIM] = yh.astype(BF16)

    chunk = RET_CHUNK
    units = [(c, hh) for c in range(rows // chunk) for hh in range(RET_HEADS)]

    def operand(c, hh, sec):
        return rin_scr[c * chunk:(c + 1) * chunk,
                       sec * GROUP_WIDTH + hh * HEAD_DIM:sec * GROUP_WIDTH + (hh + 1) * HEAD_DIM]

    masked, kv = {}, {}
    for c, hh in units:
        q, k, v = operand(c, hh, 0), operand(c, hh, 1), operand(c, hh, 2)
        sc = lax.dot_general(q, k, _NT, preferred_element_type=F32) * dec_ref[hh]
        masked[c, hh] = sc.astype(BF16)
        v_dec = (v.astype(F32) * dk_ref[hh]).astype(BF16)
        kv[c, hh] = lax.dot_general(k, v_dec, _TN, preferred_element_type=F32)
    before = {}
    for hh in range(RET_HEADS):
        state = state_ref[hh]
        for c in range(rows // chunk):
            before[c, hh] = state.astype(BF16)
            state = state * cd_ref[hh] + kv[c, hh]
        state_ref[hh] = state
    for c, hh in units:
        q, v = operand(c, hh, 0), operand(c, hh, 2)
        q_dec = (q.astype(F32) * dq_ref[hh]).astype(BF16)
        o = jnp.dot(jnp.concatenate([masked[c, hh], q_dec], axis=1),
                    jnp.concatenate([v, before[c, hh]], axis=0), preferred_element_type=F32)
        mu = jnp.mean(o, axis=-1, keepdims=True)
        oc = o - mu
        var = jnp.mean(oc * oc, axis=-1, keepdims=True)
        on = oc * lax.rsqrt(var + EPS) * rg_ref[:, hh * HEAD_DIM:(hh + 1) * HEAD_DIM]
        z = operand(c, hh, 3).astype(F32)
        ret_ref[c * chunk:(c + 1) * chunk, hh * HEAD_DIM:(hh + 1) * HEAD_DIM] = (
            z * jax.nn.sigmoid(z) * on).astype(BF16)

    for s in range(RET_SECTIONS, N_SECTIONS):
        moba_section(s)


def _retention_tables():
    c = RET_CHUNK
    log_g = np.log1p(-np.power(2.0, -5.0 - np.arange(RET_HEADS, dtype=np.float64)))
    j = np.arange(c, dtype=np.float64)
    rel = j[:, None] - j[None, :]
    decay = np.where(rel >= 0, np.exp(log_g[:, None, None] * np.maximum(rel, 0.0)[None]), 0.0)
    dq = np.exp(log_g[:, None] * (j + 1.0)[None])
    dk = np.exp(log_g[:, None] * (c - 1 - j)[None])
    cd = np.exp(log_g * c)
    wide = lambda t: np.broadcast_to(t[:, :, None], (RET_HEADS, c, HEAD_DIM))
    return (jnp.asarray(decay, F32), jnp.asarray(wide(dq), F32), jnp.asarray(wide(dk), F32),
            jnp.asarray(np.broadcast_to(cd[:, None, None], (RET_HEADS, 1, HEAD_DIM)), F32))


def _inproj(x2, layer, norm_g, w_in, cos, sin, q_g, k_g, ret_g, seq):
    m, d_model = x2.shape
    width = w_in.shape[2]
    moba_width = len(MOBA_PROJ_SECTIONS) * GROUP_WIDTH
    tm = ROW_TILE
    seq_tiles = seq // tm
    tables = _retention_tables()
    full = lambda a: pl.BlockSpec(a.shape, lambda i: (0,) * a.ndim)
    row1 = lambda n: pl.BlockSpec((None, 1, n), lambda i: (layer, 0, 0))
    return pl.pallas_call(
        functools.partial(_inproj_kernel, seq_tiles=seq_tiles),
        grid=(m // tm,),
        in_specs=[
            pl.BlockSpec((tm, d_model), lambda i: (i, 0)),
            row1(d_model),
            pl.BlockSpec((None, d_model, width), lambda i: (layer, 0, 0),
                         pipeline_mode=pl.Buffered(1)),
            pl.BlockSpec((tm, HEAD_DIM), lambda i: (i % seq_tiles, 0)),
            pl.BlockSpec((tm, HEAD_DIM), lambda i: (i % seq_tiles, 0)),
            row1(HEAD_DIM), row1(HEAD_DIM),
            *[full(t) for t in tables],
            row1(GROUP_WIDTH),
        ],
        out_specs=[
            pl.BlockSpec((tm, moba_width), lambda i: (i, 0)),
            pl.BlockSpec((tm // MOBA_BLOCK, 1, GROUP_WIDTH), lambda i: (i, 0, 0)),
            pl.BlockSpec((tm, GROUP_WIDTH), lambda i: (i, 0)),
            pl.BlockSpec((None, GROUP_WIDTH, tm), lambda i: (i, 0, 0)),
        ],
        out_shape=[
            jax.ShapeDtypeStruct((m, moba_width), BF16),
            jax.ShapeDtypeStruct((m // MOBA_BLOCK, 1, GROUP_WIDTH), F32),
            jax.ShapeDtypeStruct((m, GROUP_WIDTH), BF16),
            jax.ShapeDtypeStruct((m // tm, GROUP_WIDTH, tm), BF16),
        ],
        scratch_shapes=[pltpu.VMEM((d_model, width), BF16),
                        pltpu.VMEM((tm, RET_SECTIONS * GROUP_WIDTH), BF16),
                        pltpu.VMEM((RET_HEADS, HEAD_DIM, HEAD_DIM), F32)],
        compiler_params=pltpu.CompilerParams(
            dimension_semantics=("arbitrary",), vmem_limit_bytes=VMEM_LIMIT),
        name="in_projection",
    )(x2, norm_g, w_in, cos, sin, q_g, k_g, *tables, ret_g)


def _moba_kernel(rb_ref, q_ref, k_ref, vt_ref, z_ref, km_ref, oh_ref, bias_ref, o_ref,
                 qaug_scr, m_scr, acc_scr, s_scr):
    blk = MOBA_BLOCK
    tile = MOBA_TILE
    sub = SUBLANES
    ti = pl.program_id(1)
    n_heads = MOBA_HEADS
    sel_rows = -(-(k_ref.shape[0] // blk) // BF16_ROWS) * BF16_ROWS
    blk_id = lax.broadcasted_iota(jnp.int32, (sel_rows, tile), 0)
    blk_f = blk_id.astype(F32)
    q_blk = 2 * ti + lax.broadcasted_iota(jnp.int32, (sel_rows, tile), 1) // blk
    unused = jnp.zeros((SEL_LANES - sel_rows, tile), BF16)
    past = blk_id < q_blk
    far = (q_blk - blk_id) >= N_NEAR_BLOCKS

    for h in range(n_heads):
        cols = slice(h * HEAD_DIM, (h + 1) * HEAD_DIM)
        q = q_ref[:, cols]
        km = km_ref[0, :sel_rows, cols]
        km_hi = km.astype(BF16)
        km_lo = (km - km_hi.astype(F32)).astype(BF16)
        gate = lax.dot_general(jnp.concatenate([km_hi, km_lo], axis=1),
                               jnp.concatenate([q, q], axis=1), _NT,
                               preferred_element_type=F32)
        g = jnp.where(past, gate, MASKED_GATE)
        sel = jnp.zeros(gate.shape, F32)
        for _ in range(MOBA_TOPK):
            top = jnp.max(g, axis=0, keepdims=True)
            first = jnp.min(jnp.where(g == top, blk_f, 1e9), axis=0, keepdims=True)
            pick = blk_f == first
            sel = jnp.where(pick, 1.0, sel)
            g = jnp.where(pick, MASKED_GATE, g)
        chosen = jnp.where(past, sel, 0.0) > 0.0
        far_c = jnp.full(gate.shape, rb_ref[(N_BUCKETS - 1) * n_heads + h] * LOG2E, F32)
        far_hi = far_c.astype(BF16).astype(F32)
        mask_hi = jnp.where(chosen, jnp.where(far, far_hi, 0.0), NEG_INF)
        mask_hi = jnp.where(blk_id == q_blk, 0.0, mask_hi)
        mask_lo = jnp.where(chosen, jnp.where(far, far_c - far_hi, 0.0), 0.0)
        q_t = q.astype(F32).T.astype(BF16)
        qaug_scr[h] = jnp.concatenate(
            [q_t, mask_hi.astype(BF16), unused, mask_lo.astype(BF16), unused], axis=0)

    m_scr[...] = jnp.full(m_scr.shape, NEG_INF, F32)
    ones_rows = jnp.ones((BF16_ROWS, tile), BF16)
    acc_scr[...] = jnp.zeros(acc_scr.shape, F32)

    def key_rows(step):
        return pl.ds(pl.multiple_of((ti - step) * tile, tile), tile)

    def bias_tile(h, step):
        def part(offset):
            idx = jnp.where(offset < 0, N_NEAR_BLOCKS, jnp.minimum(offset, N_NEAR_BLOCKS))
            return bias_ref[h, idx]
        diag = part(2 * step)
        return jnp.concatenate([jnp.concatenate([diag, part(2 * step + 1)], axis=1),
                                jnp.concatenate([part(2 * step - 1), diag], axis=1)], axis=0)

    def scores(step, slot, h, with_bias):
        rows = key_rows(step)
        k_aug = jnp.concatenate([k_ref[rows, h * HEAD_DIM:(h + 1) * HEAD_DIM], oh_ref[rows, :]], axis=1)
        s = jnp.dot(k_aug, qaug_scr[h], preferred_element_type=F32)
        if with_bias:
            s = s + bias_tile(h, step)
        s_scr[slot, h] = s

    def accumulate(step, slot, h):
        s = s_scr[slot, h].reshape(tile // sub, sub, tile)
        m_prev = m_scr[h]
        m_tile = jnp.max(jnp.max(s, axis=0), axis=0, keepdims=True)
        m_next = jnp.maximum(m_prev, jnp.broadcast_to(m_tile, m_prev.shape))
        alpha = jnp.exp2(m_prev - m_next)
        p = jnp.exp2(s - m_next[None])
        m_scr[h] = m_next
        vt_aug = jnp.concatenate([vt_ref[ti - step, h * HEAD_DIM:(h + 1) * HEAD_DIM, :], ones_rows],
                                 axis=0)
        pv = jnp.dot(vt_aug, p.reshape(tile, tile).astype(BF16),
                     preferred_element_type=F32)
        acc = acc_scr[h].reshape(ACC_ROWS // sub, sub, tile) * alpha[None]
        acc_scr[h] = acc.reshape(ACC_ROWS, tile) + pv

    def loop_body(next_with_bias):
        def one_step(step, slot):
            for h in range(n_heads):
                scores(step + 1, 1 - slot, h, next_with_bias)
                accumulate(step, slot, h)

        def body(step, carry):
            for slot in (0, 1):
                pl.when((step & 1) == slot)(functools.partial(one_step, step, slot))
            return carry
        return body

    n_biased = jnp.minimum(ti, MOBA_NEAR_STEPS - 1)
    for h in range(n_heads):
        scores(0, 0, h, True)
    lax.fori_loop(0, n_biased, loop_body(True), 0)
    lax.fori_loop(n_biased, ti, loop_body(False), 0)
    for h in range(n_heads):
        accumulate(ti, ti & 1, h)
    for h in range(n_heads):
        cols = slice(h * HEAD_DIM, (h + 1) * HEAD_DIM)
        z = z_ref[:, cols].astype(F32)
        denom = acc_scr[h, HEAD_DIM:HEAD_DIM + 1, :]
        out = (acc_scr[h, :HEAD_DIM, :] / denom).T
        o_ref[:, cols] = (z * jax.nn.sigmoid(z) * out).astype(BF16)


def _moba(proj, v_t, kmean, bias_tiles, rel_bias, batch, seq):
    m = proj.shape[0]
    tile = MOBA_TILE
    nq = seq // tile
    row_blk = np.arange(seq)[:, None] // MOBA_BLOCK
    onehot = jnp.asarray((np.arange(HEAD_DIM)[None, :] & (SEL_LANES - 1)) == row_blk, BF16)
    rows = lambda sec: pl.BlockSpec((tile, GROUP_WIDTH), lambda b, i: (b * nq + i, sec))
    whole = lambda sec: pl.BlockSpec((seq, GROUP_WIDTH), lambda b, i: (b, sec))
    return pl.pallas_call(
        _moba_kernel,
        grid=(batch, nq),
        in_specs=[pl.BlockSpec(memory_space=pltpu.SMEM),
                  rows(0), whole(1),
                  pl.BlockSpec((nq, GROUP_WIDTH, tile), lambda b, i: (b, 0, 0),
                               pipeline_mode=pl.Buffered(1)),
                  rows(2),
                  pl.BlockSpec((1, SEL_LANES, GROUP_WIDTH), lambda b, i: (b, 0, 0)),
                  pl.BlockSpec(onehot.shape, lambda b, i: (0, 0), pipeline_mode=pl.Buffered(1)),
                  pl.BlockSpec(bias_tiles.shape, lambda b, i: (0, 0, 0, 0),
                               pipeline_mode=pl.Buffered(1))],
        out_specs=pl.BlockSpec((tile, GROUP_WIDTH), lambda b, i: (b * nq + i, 0)),
        out_shape=jax.ShapeDtypeStruct((m, GROUP_WIDTH), BF16),
        scratch_shapes=[pltpu.VMEM((MOBA_HEADS, 2 * HEAD_DIM, tile), BF16),
                        pltpu.VMEM((MOBA_HEADS, SUBLANES, tile), F32),
                        pltpu.VMEM((MOBA_HEADS, ACC_ROWS, tile), F32),
                        pltpu.VMEM((2, MOBA_HEADS, tile, tile), F32)],
        compiler_params=pltpu.CompilerParams(
            dimension_semantics=("arbitrary", "arbitrary"), vmem_limit_bytes=VMEM_LIMIT),
        name="moba",
    )(rel_bias.astype(F32).reshape(-1), proj, proj, v_t, proj, kmean, onehot, bias_tiles)


def _outproj_kernel(x_ref, r_ref, a_ref, wo_ref, p_ref, wp_ref, pg_ref, wg_ref, o_ref,
                    wo_scr, wp_scr, wg_scr):
    _cast_weights_once([(wo_ref, wo_scr), (wp_ref, wp_scr), (wg_ref, wg_scr)])

    def unit_rms(t):
        return t * lax.rsqrt(jnp.mean(t * t, axis=-1, keepdims=True) + EPS)

    half = x_ref.shape[0] // 2
    halves = [slice(0, half), slice(half, 2 * half)]
    for rows in halves:
        y = (jnp.dot(r_ref[rows, :], wo_scr[:GROUP_WIDTH, :], preferred_element_type=F32)
             + jnp.dot(a_ref[rows, :], wo_scr[GROUP_WIDTH:, :], preferred_element_type=F32))
        o_ref[rows, :] = x_ref[rows, :] + y
    for rows in halves:
        e = unit_rms(jnp.dot(p_ref[rows, :].astype(BF16), wp_scr[...], preferred_element_type=F32))
        x1 = o_ref[rows, :]
        gate = jax.nn.sigmoid(jnp.dot(unit_rms(x1).astype(BF16), wg_scr[...],
                                      preferred_element_type=F32))
        o_ref[rows, :] = x1 + gate * (e * pg_ref[...])


def _outproj(x2, r_out, m_out, layer, w_out, p3, w_ple, ple_g, w_gate):
    m, d_model = x2.shape
    tm = OUT_ROW_TILE
    rows = lambda a: pl.BlockSpec((tm, a.shape[1]), lambda i: (i, 0))
    full = lambda a: pl.BlockSpec((None,) + a.shape[1:], lambda i: (layer, 0, 0),
                                  pipeline_mode=pl.Buffered(1))
    return pl.pallas_call(
        _outproj_kernel,
        grid=(m // tm,),
        in_specs=[rows(x2), rows(r_out), rows(m_out), full(w_out),
                  pl.BlockSpec((None, tm, p3.shape[2]), lambda i: (layer, i, 0)), full(w_ple),
                  full(ple_g), full(w_gate)],
        out_specs=pl.BlockSpec((tm, d_model), lambda i: (i, 0)),
        out_shape=jax.ShapeDtypeStruct((m, d_model), F32),
        scratch_shapes=[pltpu.VMEM(w.shape[1:], BF16) for w in (w_out, w_ple, w_gate)],
        compiler_params=pltpu.CompilerParams(
            dimension_semantics=("arbitrary",), vmem_limit_bytes=VMEM_LIMIT),
        name="out_projection",
    )(x2, r_out, m_out, w_out, p3, w_ple, ple_g, w_gate)


def _rotary_tables(seq):
    half = HEAD_DIM // 2
    inv = np.float32(ROPE_THETA) ** (-np.arange(half, dtype=np.float32) / np.float32(half))
    ang = np.arange(seq, dtype=np.float32)[:, None] * inv[None, :]
    cos, sin = np.cos(ang), np.sin(ang)
    return (jnp.asarray(np.concatenate([cos, cos], axis=1), F32),
            jnp.asarray(np.concatenate([-sin, sin], axis=1), F32))


def kernel(x, p, norm_g, w_in, ret_norm_g, q_norm_g, k_norm_g, rel_bias, w_out, w_ple, ple_norm_g, w_ple_gate):
    batch, seq, d_model = x.shape
    depth = w_in.shape[0]
    m = batch * seq
    assert w_in.shape[2] == N_SECTIONS * GROUP_WIDTH and w_out.shape[1] == 2 * GROUP_WIDTH
    assert rel_bias.shape == (N_BUCKETS, MOBA_HEADS)
    assert seq % ROW_TILE == 0 and ROW_TILE % MOBA_BLOCK == 0 and seq % MOBA_TILE == 0
    assert m % OUT_ROW_TILE == 0
    assert ROW_TILE == MOBA_TILE
    assert ROW_TILE % RET_CHUNK == 0 and seq // MOBA_BLOCK <= SEL_LANES

    cos, sin = _rotary_tables(seq)
    bias_tiles = _bias_tiles(rel_bias)
    x2 = x.reshape(m, d_model)
    nb = seq // MOBA_BLOCK
    stack = lambda g: g.reshape(depth, 1, -1)
    p3 = p.reshape(depth, m, -1)
    for i in range(depth):
        proj, kmean, r_out, v_t = _inproj(x2, i, stack(norm_g), w_in, cos, sin,
                                     stack(q_norm_g), stack(k_norm_g), stack(ret_norm_g), seq)
        kmean = jnp.pad(kmean.reshape(batch, nb, GROUP_WIDTH), ((0, 0), (0, SEL_LANES - nb), (0, 0)))
        m_out = _moba(proj, v_t, kmean, bias_tiles, rel_bias, batch, seq)
        x2 = _outproj(x2, r_out, m_out, i, w_out, p3, w_ple, stack(ple_norm_g), w_ple_gate)
    return x2.reshape(batch, seq, d_model)
```

```python
import functools
import math

import numpy as np
import jax
import jax.numpy as jnp
from jax import lax
from jax.experimental import pallas as pl
from jax.experimental.pallas import tpu as pltpu

F32 = jnp.float32
BF16 = jnp.bfloat16

HEAD_DIM = 128
SUBLANES = 8
BF16_ROWS = 16
ACC_ROWS = HEAD_DIM + BF16_ROWS
RET_HEADS = 4
MOBA_HEADS = 4
GROUP_WIDTH = 512
N_SECTIONS = 8
RET_SECTIONS = 4
MOBA_V_SECTION = 6
MOBA_PROJ_SECTIONS = (4, 5, 7)
RET_CHUNK = 128
MOBA_BLOCK = 256
MOBA_TOPK = 3
N_BUCKETS = 32
MAX_DISTANCE = 2048
ROPE_THETA = 10000.0
EPS = 1e-6
NEG_INF = -1e30
MASKED_GATE = -3e38
LOG2E = math.log2(math.e)
SEL_LANES = HEAD_DIM // 2

ROW_TILE = 512
OUT_ROW_TILE = 1024
CAST_ROWS = 256
VMEM_LIMIT = 56 * 1024 * 1024

_NT = (((1,), (1,)), ((), ()))


def _bucket_thresholds():
    max_exact = N_BUCKETS // 2
    thr = []
    for k in range(N_BUCKETS - max_exact):
        n = max_exact
        while min(max_exact + int(math.log(n / max_exact) / math.log(MAX_DISTANCE / max_exact)
                                  * (N_BUCKETS - max_exact)), N_BUCKETS - 1) < max_exact + k:
            n += 1
        thr.append(n)
    return thr


_BUCKET_THR = _bucket_thresholds()
N_NEAR_BLOCKS = next(d for d in range(1, 1 << 20) if (d - 1) * MOBA_BLOCK + 1 >= _BUCKET_THR[-1])
MOBA_TILE = 2 * MOBA_BLOCK
MOBA_NEAR_STEPS = N_NEAR_BLOCKS // 2 + 1


def _bias_kernel(rb_ref, out_ref):
    h = pl.program_id(0)
    n_heads = pl.num_programs(0)
    c = lax.broadcasted_iota(jnp.int32, (MOBA_BLOCK, MOBA_BLOCK), 0)
    r = lax.broadcasted_iota(jnp.int32, (MOBA_BLOCK, MOBA_BLOCK), 1)
    max_exact = N_BUCKETS // 2

    def rb(b):
        return rb_ref[b * n_heads + h] * LOG2E

    for d in range(N_NEAR_BLOCKS):
        n = d * MOBA_BLOCK + r - c
        nn = jnp.maximum(n, 0)
        val = jnp.full((MOBA_BLOCK, MOBA_BLOCK), rb(0), F32)
        for b in range(1, max_exact):
            val = jnp.where(nn >= b, rb(b), val)
        for k, thr in enumerate(_BUCKET_THR):
            val = jnp.where(nn >= thr, rb(max_exact + k), val)
        if d == 0:
            val = jnp.where(n >= 0, val, NEG_INF)
        out_ref[0, d] = val
    out_ref[0, N_NEAR_BLOCKS] = jnp.zeros((MOBA_BLOCK, MOBA_BLOCK), F32)


def _bias_tiles(rel_bias):
    n_heads = rel_bias.shape[1]
    return pl.pallas_call(
        _bias_kernel,
        grid=(n_heads,),
        in_specs=[pl.BlockSpec(memory_space=pltpu.SMEM)],
        out_specs=pl.BlockSpec((1, N_NEAR_BLOCKS + 1, MOBA_BLOCK, MOBA_BLOCK), lambda h: (h, 0, 0, 0)),
        out_shape=jax.ShapeDtypeStruct((n_heads, N_NEAR_BLOCKS + 1, MOBA_BLOCK, MOBA_BLOCK), F32),
        name="bias_tiles",
    )(rel_bias.astype(F32).reshape(-1))


def _cast_weights_once(pairs):
    @pl.when(pl.program_id(0) == 0)
    def _():
        for w_ref, w_scr in pairs:
            for r in range(0, w_ref.shape[0], CAST_ROWS):
                w_scr[r:r + CAST_ROWS, :] = w_ref[r:r + CAST_ROWS, :].astype(BF16)


def _inproj_kernel(x_ref, g_ref, w_ref, cos_ref, sin_ref, qg_ref, kg_ref,
                   dec_ref, dq_ref, dk_ref, cd_ref, rg_ref,
                   proj_ref, kmean_ref, ret_ref, vt_ref, w_scr, rin_scr, kt_scr, state_ref,
                   *, seq_tiles):
    _cast_weights_once([(w_ref, w_scr)])
    x = x_ref[...]
    ms = jnp.mean(x * x, axis=-1, keepdims=True)
    h = (x * lax.rsqrt(ms + EPS) * g_ref[...]).astype(BF16)
    cos = cos_ref[...]
    sin = sin_ref[...]
    scale = HEAD_DIM ** -0.5
    rows = x.shape[0]
    heads = GROUP_WIDTH // HEAD_DIM

    def head_norm(y, gain):
        return y * lax.rsqrt(jnp.mean(y * y, axis=-1, keepdims=True) + EPS) * gain

    def section(s):
        return jnp.dot(h, w_scr[:, s * GROUP_WIDTH:(s + 1) * GROUP_WIDTH], preferred_element_type=F32)

    for s in range(RET_SECTIONS):
        y = section(s)
        for hh in range(heads):
            yh = y[:, hh * HEAD_DIM:(hh + 1) * HEAD_DIM]
            if s in (0, 1):
                yh = yh * cos + pltpu.roll(yh, HEAD_DIM // 2, axis=1) * sin
                if s == 1:
                    yh = yh * scale
            if s == 1:
                kt_scr[hh] = yh.T.astype(BF16)
            else:
                col = s * GROUP_WIDTH + hh * HEAD_DIM
                rin_scr[:, col:col + HEAD_DIM] = yh.astype(BF16)

    @pl.when(pl.program_id(0) % seq_tiles == 0)
    def _():
        state_ref[...] = jnp.zeros_like(state_ref)

    def moba_section(s):
        y = section(s)
        for hh in range(heads):
            yh = y[:, hh * HEAD_DIM:(hh + 1) * HEAD_DIM]
            if s == 4:
                yh = head_norm(yh, qg_ref[...]) * (scale * LOG2E)
            elif s == 5:
                yh = head_norm(yh, kg_ref[...])
                for blk in range(rows // MOBA_BLOCK):
                    kmean_ref[blk, :, hh * HEAD_DIM:(hh + 1) * HEAD_DIM] = jnp.mean(
                        yh[blk * MOBA_BLOCK:(blk + 1) * MOBA_BLOCK], axis=0, keepdims=True)
            if s == MOBA_V_SECTION:
                vt_ref[hh * HEAD_DIM:(hh + 1) * HEAD_DIM, :] = yh.T.astype(BF16)
            else:
                col = MOBA_PROJ_SECTIONS.index(s) * GROUP_WIDTH + hh * HEAD_DIM
                proj_ref[:, col:col + HEAD_DIM] = yh.astype(BF16)

    chunk = RET_CHUNK
    units = [(c, hh) for c in range(rows // chunk) for hh in range(RET_HEADS)]

    def operand(c, hh, sec):
        return rin_scr[c * chunk:(c + 1) * chunk,
                       sec * GROUP_WIDTH + hh * HEAD_DIM:sec * GROUP_WIDTH + (hh + 1) * HEAD_DIM]

    masked, kv = {}, {}
    for c, hh in units:
        q, v = operand(c, hh, 0), operand(c, hh, 2)
        k_t = kt_scr[hh, :, c * chunk:(c + 1) * chunk]
        sc = jnp.dot(q, k_t, preferred_element_type=F32) * dec_ref[hh]
        masked[c, hh] = sc.astype(BF16)
        v_dec = (v.astype(F32) * dk_ref[hh]).astype(BF16)
        kv[c, hh] = jnp.dot(k_t, v_dec, preferred_element_type=F32)
    before = {}
    for hh in range(RET_HEADS):
        state = state_ref[hh]
        for c in range(rows // chunk):
            before[c, hh] = state.astype(BF16)
            state = state * cd_ref[hh] + kv[c, hh]
        state_ref[hh] = state
    for c, hh in units:
        q, v = operand(c, hh, 0), operand(c, hh, 2)
        q_dec = (q.astype(F32) * dq_ref[hh]).astype(BF16)
        o = jnp.dot(jnp.concatenate([masked[c, hh], q_dec], axis=1),
                    jnp.concatenate([v, before[c, hh]], axis=0), preferred_element_type=F32)
        mu = jnp.mean(o, axis=-1, keepdims=True)
        oc = o - mu
        var = jnp.mean(oc * oc, axis=-1, keepdims=True)
        on = oc * lax.rsqrt(var + EPS) * rg_ref[:, hh * HEAD_DIM:(hh + 1) * HEAD_DIM]
        z = operand(c, hh, 3).astype(F32)
        ret_ref[c * chunk:(c + 1) * chunk, hh * HEAD_DIM:(hh + 1) * HEAD_DIM] = (
            z * jax.nn.sigmoid(z) * on).astype(BF16)

    for s in range(RET_SECTIONS, N_SECTIONS):
        moba_section(s)


def _retention_tables():
    c = RET_CHUNK
    log_g = np.log1p(-np.power(2.0, -5.0 - np.arange(RET_HEADS, dtype=np.float64)))
    j = np.arange(c, dtype=np.float64)
    rel = j[:, None] - j[None, :]
    decay = np.where(rel >= 0, np.exp(log_g[:, None, None] * np.maximum(rel, 0.0)[None]), 0.0)
    dq = np.exp(log_g[:, None] * (j + 1.0)[None])
    dk = np.exp(log_g[:, None] * (c - 1 - j)[None])
    cd = np.exp(log_g * c)
    wide = lambda t: np.broadcast_to(t[:, :, None], (RET_HEADS, c, HEAD_DIM))
    return (jnp.asarray(decay, F32), jnp.asarray(wide(dq), F32), jnp.asarray(wide(dk), F32),
            jnp.asarray(np.broadcast_to(cd[:, None, None], (RET_HEADS, 1, HEAD_DIM)), F32))


def _inproj(x2, layer, norm_g, w_in, cos, sin, q_g, k_g, ret_g, seq):
    m, d_model = x2.shape
    width = w_in.shape[2]
    moba_width = len(MOBA_PROJ_SECTIONS) * GROUP_WIDTH
    tm = ROW_TILE
    seq_tiles = seq // tm
    tables = _retention_tables()
    full = lambda a: pl.BlockSpec(a.shape, lambda i: (0,) * a.ndim)
    row1 = lambda n: pl.BlockSpec((None, 1, n), lambda i: (layer, 0, 0))
    return pl.pallas_call(
        functools.partial(_inproj_kernel, seq_tiles=seq_tiles),
        grid=(m // tm,),
        in_specs=[
            pl.BlockSpec((tm, d_model), lambda i: (i, 0)),
            row1(d_model),
            pl.BlockSpec((None, d_model, width), lambda i: (layer, 0, 0),
                         pipeline_mode=pl.Buffered(1)),
            pl.BlockSpec((tm, HEAD_DIM), lambda i: (i % seq_tiles, 0)),
            pl.BlockSpec((tm, HEAD_DIM), lambda i: (i % seq_tiles, 0)),
            row1(HEAD_DIM), row1(HEAD_DIM),
            *[full(t) for t in tables],
            row1(GROUP_WIDTH),
        ],
        out_specs=[
            pl.BlockSpec((tm, moba_width), lambda i: (i, 0)),
            pl.BlockSpec((tm // MOBA_BLOCK, 1, GROUP_WIDTH), lambda i: (i, 0, 0)),
            pl.BlockSpec((tm, GROUP_WIDTH), lambda i: (i, 0)),
            pl.BlockSpec((None, GROUP_WIDTH, tm), lambda i: (i, 0, 0)),
        ],
        out_shape=[
            jax.ShapeDtypeStruct((m, moba_width), BF16),
            jax.ShapeDtypeStruct((m // MOBA_BLOCK, 1, GROUP_WIDTH), F32),
            jax.ShapeDtypeStruct((m, GROUP_WIDTH), BF16),
            jax.ShapeDtypeStruct((m // tm, GROUP_WIDTH, tm), BF16),
        ],
        scratch_shapes=[pltpu.VMEM((d_model, width), BF16),
                        pltpu.VMEM((tm, RET_SECTIONS * GROUP_WIDTH), BF16),
                        pltpu.VMEM((RET_HEADS, HEAD_DIM, tm), BF16),
                        pltpu.VMEM((RET_HEADS, HEAD_DIM, HEAD_DIM), F32)],
        compiler_params=pltpu.CompilerParams(
            dimension_semantics=("arbitrary",), vmem_limit_bytes=VMEM_LIMIT),
        name="in_projection",
    )(x2, norm_g, w_in, cos, sin, q_g, k_g, *tables, ret_g)


def _moba_kernel(rb_ref, q_ref, k_ref, vt_ref, z_ref, km_ref, oh_ref, bias_ref, o_ref,
                 qaug_scr, m_scr, acc_scr, s_scr):
    blk = MOBA_BLOCK
    tile = MOBA_TILE
    sub = SUBLANES
    ti = pl.program_id(1)
    n_heads = MOBA_HEADS
    sel_rows = -(-(k_ref.shape[0] // blk) // BF16_ROWS) * BF16_ROWS
    blk_id = lax.broadcasted_iota(jnp.int32, (sel_rows, tile), 0)
    blk_f = blk_id.astype(F32)
    q_blk = 2 * ti + lax.broadcasted_iota(jnp.int32, (sel_rows, tile), 1) // blk
    unused = jnp.zeros((SEL_LANES - sel_rows, tile), BF16)
    past = blk_id < q_blk
    far = (q_blk - blk_id) >= N_NEAR_BLOCKS

    for h in range(n_heads):
        cols = slice(h * HEAD_DIM, (h + 1) * HEAD_DIM)
        q = q_ref[:, cols]
        km = km_ref[0, :sel_rows, cols]
        km_hi = km.astype(BF16)
        km_lo = (km - km_hi.astype(F32)).astype(BF16)
        gate = lax.dot_general(jnp.concatenate([km_hi, km_lo], axis=1),
                               jnp.concatenate([q, q], axis=1), _NT,
                               preferred_element_type=F32)
        g = jnp.where(past, gate, MASKED_GATE)
        sel = jnp.zeros(gate.shape, F32)
        for _ in range(MOBA_TOPK):
            top = jnp.max(g, axis=0, keepdims=True)
            first = jnp.min(jnp.where(g == top, blk_f, 1e9), axis=0, keepdims=True)
            pick = blk_f == first
            sel = jnp.where(pick, 1.0, sel)
            g = jnp.where(pick, MASKED_GATE, g)
        chosen = jnp.where(past, sel, 0.0) > 0.0
        far_c = jnp.full(gate.shape, rb_ref[(N_BUCKETS - 1) * n_heads + h] * LOG2E, F32)
        far_hi = far_c.astype(BF16).astype(F32)
        mask_hi = jnp.where(chosen, jnp.where(far, far_hi, 0.0), NEG_INF)
        mask_hi = jnp.where(blk_id == q_blk, 0.0, mask_hi)
        mask_lo = jnp.where(chosen, jnp.where(far, far_c - far_hi, 0.0), 0.0)
        q_t = q.astype(F32).T.astype(BF16)
        qaug_scr[h] = jnp.concatenate(
            [q_t, mask_hi.astype(BF16), unused, mask_lo.astype(BF16), unused], axis=0)

    m_scr[...] = jnp.full(m_scr.shape, NEG_INF, F32)
    ones_rows = jnp.ones((BF16_ROWS, tile), BF16)
    acc_scr[...] = jnp.zeros(acc_scr.shape, F32)

    def key_rows(step):
        return pl.ds(pl.multiple_of((ti - step) * tile, tile), tile)

    def bias_tile(h, step):
        def part(offset):
            idx = jnp.where(offset < 0, N_NEAR_BLOCKS, jnp.minimum(offset, N_NEAR_BLOCKS))
            return bias_ref[h, idx]
        diag = part(2 * step)
        return jnp.concatenate([jnp.concatenate([diag, part(2 * step + 1)], axis=1),
                                jnp.concatenate([part(2 * step - 1), diag], axis=1)], axis=0)

    def scores(step, slot, h, with_bias):
        rows = key_rows(step)
        k_aug = jnp.concatenate([k_ref[rows, h * HEAD_DIM:(h + 1) * HEAD_DIM], oh_ref[rows, :]], axis=1)
        s = jnp.dot(k_aug, qaug_scr[h], preferred_element_type=F32)
        if with_bias:
            s = s + bias_tile(h, step)
        s_scr[slot, h] = s

    def accumulate(step, slot, h):
        s = s_scr[slot, h].reshape(tile // sub, sub, tile)
        m_prev = m_scr[h]
        m_tile = jnp.max(jnp.max(s, axis=0), axis=0, keepdims=True)
        m_next = jnp.maximum(m_prev, jnp.broadcast_to(m_tile, m_prev.shape))
        alpha = jnp.exp2(m_prev - m_next)
        p = jnp.exp2(s - m_next[None])
        m_scr[h] = m_next
        vt_aug = jnp.concatenate([vt_ref[ti - step, h * HEAD_DIM:(h + 1) * HEAD_DIM, :], ones_rows],
                                 axis=0)
        pv = jnp.dot(vt_aug, p.reshape(tile, tile).astype(BF16),
                     preferred_element_type=F32)
        acc = acc_scr[h].reshape(ACC_ROWS // sub, sub, tile) * alpha[None]
        acc_scr[h] = acc.reshape(ACC_ROWS, tile) + pv

    def loop_body(next_with_bias):
        def one_step(step, slot):
            for h in range(n_heads):
                scores(step + 1, 1 - slot, h, next_with_bias)
                accumulate(step, slot, h)

        def body(step, carry):
            for slot in (0, 1):
                pl.when((step & 1) == slot)(functools.partial(one_step, step, slot))
            return carry
        return body

    n_biased = jnp.minimum(ti, MOBA_NEAR_STEPS - 1)
    for h in range(n_heads):
        scores(0, 0, h, True)
    lax.fori_loop(0, n_biased, loop_body(True), 0)
    lax.fori_loop(n_biased, ti, loop_body(False), 0)
    for h in range(n_heads):
        accumulate(ti, ti & 1, h)
    for h in range(n_heads):
        cols = slice(h * HEAD_DIM, (h + 1) * HEAD_DIM)
        z = z_ref[:, cols].astype(F32)
        inv = 1.0 / acc_scr[h, HEAD_DIM:HEAD_DIM + 1, :]
        out = (acc_scr[h, :HEAD_DIM, :] * inv).T
        o_ref[:, cols] = (z * jax.nn.sigmoid(z) * out).astype(BF16)


def _moba(proj, v_t, kmean, bias_tiles, rel_bias, batch, seq):
    m = proj.shape[0]
    tile = MOBA_TILE
    nq = seq // tile
    row_blk = np.arange(seq)[:, None] // MOBA_BLOCK
    onehot = jnp.asarray((np.arange(HEAD_DIM)[None, :] & (SEL_LANES - 1)) == row_blk, BF16)
    rows = lambda sec: pl.BlockSpec((tile, GROUP_WIDTH), lambda b, i: (b * nq + i, sec))
    whole = lambda sec: pl.BlockSpec((seq, GROUP_WIDTH), lambda b, i: (b, sec))
    return pl.pallas_call(
        _moba_kernel,
        grid=(batch, nq),
        in_specs=[pl.BlockSpec(memory_space=pltpu.SMEM),
                  rows(0), whole(1),
                  pl.BlockSpec((nq, GROUP_WIDTH, tile), lambda b, i: (b, 0, 0),
                               pipeline_mode=pl.Buffered(1)),
                  rows(2),
                  pl.BlockSpec((1, SEL_LANES, GROUP_WIDTH), lambda b, i: (b, 0, 0)),
                  pl.BlockSpec(onehot.shape, lambda b, i: (0, 0), pipeline_mode=pl.Buffered(1)),
                  pl.BlockSpec(bias_tiles.shape, lambda b, i: (0, 0, 0, 0),
                               pipeline_mode=pl.Buffered(1))],
        out_specs=pl.BlockSpec((tile, GROUP_WIDTH), lambda b, i: (b * nq + i, 0)),
        out_shape=jax.ShapeDtypeStruct((m, GROUP_WIDTH), BF16),
        scratch_shapes=[pltpu.VMEM((MOBA_HEADS, 2 * HEAD_DIM, tile), BF16),
                        pltpu.VMEM((MOBA_HEADS, SUBLANES, tile), F32),
                        pltpu.VMEM((MOBA_HEADS, ACC_ROWS, tile), F32),
                        pltpu.VMEM((2, MOBA_HEADS, tile, tile), F32)],
        compiler_params=pltpu.CompilerParams(
            dimension_semantics=("arbitrary", "arbitrary"), vmem_limit_bytes=VMEM_LIMIT),
        name="moba",
    )(rel_bias.astype(F32).reshape(-1), proj, proj, v_t, proj, kmean, onehot, bias_tiles)


def _outproj_kernel(x_ref, r_ref, a_ref, wo_ref, p_ref, wp_ref, pg_ref, wg_ref, o_ref,
                    wo_scr, wp_scr, wg_scr):
    _cast_weights_once([(wo_ref, wo_scr), (wp_ref, wp_scr), (wg_ref, wg_scr)])

    def unit_rms(t):
        return t * lax.rsqrt(jnp.mean(t * t, axis=-1, keepdims=True) + EPS)

    half = x_ref.shape[0] // 2
    halves = [slice(0, half), slice(half, 2 * half)]
    for rows in halves:
        y = (jnp.dot(r_ref[rows, :], wo_scr[:GROUP_WIDTH, :], preferred_element_type=F32)
             + jnp.dot(a_ref[rows, :], wo_scr[GROUP_WIDTH:, :], preferred_element_type=F32))
        o_ref[rows, :] = x_ref[rows, :] + y
    for rows in halves:
        e = unit_rms(jnp.dot(p_ref[rows, :].astype(BF16), wp_scr[...], preferred_element_type=F32))
        x1 = o_ref[rows, :]
        gate = jax.nn.sigmoid(jnp.dot(unit_rms(x1).astype(BF16), wg_scr[...],
                                      preferred_element_type=F32))
        o_ref[rows, :] = x1 + gate * (e * pg_ref[...])


def _outproj(x2, r_out, m_out, layer, w_out, p3, w_ple, ple_g, w_gate):
    m, d_model = x2.shape
    tm = OUT_ROW_TILE
    rows = lambda a: pl.BlockSpec((tm, a.shape[1]), lambda i: (i, 0))
    full = lambda a: pl.BlockSpec((None,) + a.shape[1:], lambda i: (layer, 0, 0),
                                  pipeline_mode=pl.Buffered(1))
    return pl.pallas_call(
        _outproj_kernel,
        grid=(m // tm,),
        in_specs=[rows(x2), rows(r_out), rows(m_out), full(w_out),
                  pl.BlockSpec((None, tm, p3.shape[2]), lambda i: (layer, i, 0)), full(w_ple),
                  full(ple_g), full(w_gate)],
        out_specs=pl.BlockSpec((tm, d_model), lambda i: (i, 0)),
        out_shape=jax.ShapeDtypeStruct((m, d_model), F32),
        scratch_shapes=[pltpu.VMEM(w.shape[1:], BF16) for w in (w_out, w_ple, w_gate)],
        compiler_params=pltpu.CompilerParams(
            dimension_semantics=("arbitrary",), vmem_limit_bytes=VMEM_LIMIT),
        name="out_projection",
    )(x2, r_out, m_out, w_out, p3, w_ple, ple_g, w_gate)


def _rotary_tables(seq):
    half = HEAD_DIM // 2
    inv = np.float32(ROPE_THETA) ** (-np.arange(half, dtype=np.float32) / np.float32(half))
    ang = np.arange(seq, dtype=np.float32)[:, None] * inv[None, :]
    cos, sin = np.cos(ang), np.sin(ang)
    return (jnp.asarray(np.concatenate([cos, cos], axis=1), F32),
            jnp.asarray(np.concatenate([-sin, sin], axis=1), F32))


def kernel(x, p, norm_g, w_in, ret_norm_g, q_norm_g, k_norm_g, rel_bias, w_out, w_ple, ple_norm_g, w_ple_gate):
    batch, seq, d_model = x.shape
    depth = w_in.shape[0]
    m = batch * seq
    assert w_in.shape[2] == N_SECTIONS * GROUP_WIDTH and w_out.shape[1] == 2 * GROUP_WIDTH
    assert rel_bias.shape == (N_BUCKETS, MOBA_HEADS)
    assert seq % ROW_TILE == 0 and ROW_TILE % MOBA_BLOCK == 0 and seq % MOBA_TILE == 0
    assert m % OUT_ROW_TILE == 0
    assert ROW_TILE == MOBA_TILE
    assert ROW_TILE % RET_CHUNK == 0 and seq // MOBA_BLOCK <= SEL_LANES

    cos, sin = _rotary_tables(seq)
    bias_tiles = _bias_tiles(rel_bias)
    x2 = x.reshape(m, d_model)
    nb = seq // MOBA_BLOCK
    stack = lambda g: g.reshape(depth, 1, -1)
    p3 = p.reshape(depth, m, -1)
    for i in range(depth):
        proj, kmean, r_out, v_t = _inproj(x2, i, stack(norm_g), w_in, cos, sin,
                                     stack(q_norm_g), stack(k_norm_g), stack(ret_norm_g), seq)
        kmean = jnp.pad(kmean.reshape(batch, nb, GROUP_WIDTH), ((0, 0), (0, SEL_LANES - nb), (0, 0)))
        m_out = _moba(proj, v_t, kmean, bias_tiles, rel_bias, batch, seq)
        x2 = _outproj(x2, r_out, m_out, i, w_out, p3, w_ple, stack(ple_norm_g), w_ple_gate)
    return x2.reshape(batch, seq, d_model)
```

```python
import functools
import math

import numpy as np
import jax
import jax.numpy as jnp
from jax import lax
from jax.experimental import pallas as pl
from jax.experimental.pallas import tpu as pltpu

F32 = jnp.float32
BF16 = jnp.bfloat16

HEAD_DIM = 128
SUBLANES = 8
BF16_ROWS = 16
ACC_ROWS = HEAD_DIM + BF16_ROWS
RET_HEADS = 4
MOBA_HEADS = 4
GROUP_WIDTH = 512
N_SECTIONS = 8
RET_SECTIONS = 4
MOBA_V_SECTION = 6
MOBA_PROJ_SECTIONS = (4, 5, 7)
RET_CHUNK = 128
MOBA_BLOCK = 256
MOBA_TOPK = 3
N_BUCKETS = 32
MAX_DISTANCE = 2048
ROPE_THETA = 10000.0
EPS = 1e-6
NEG_INF = -1e30
MASKED_GATE = -3e38
LOG2E = math.log2(math.e)
SEL_LANES = HEAD_DIM // 2

ROW_TILE = 512
OUT_ROW_TILE = 1024
OUT_SLABS = 4
CAST_ROWS = 256
VMEM_LIMIT = 56 * 1024 * 1024

_NT = (((1,), (1,)), ((), ()))


def _bucket_thresholds():
    max_exact = N_BUCKETS // 2
    thr = []
    for k in range(N_BUCKETS - max_exact):
        n = max_exact
        while min(max_exact + int(math.log(n / max_exact) / math.log(MAX_DISTANCE / max_exact)
                                  * (N_BUCKETS - max_exact)), N_BUCKETS - 1) < max_exact + k:
            n += 1
        thr.append(n)
    return thr


_BUCKET_THR = _bucket_thresholds()
N_NEAR_BLOCKS = next(d for d in range(1, 1 << 20) if (d - 1) * MOBA_BLOCK + 1 >= _BUCKET_THR[-1])
MOBA_TILE = 2 * MOBA_BLOCK
MOBA_NEAR_STEPS = N_NEAR_BLOCKS // 2 + 1


def _bias_kernel(rb_ref, out_ref):
    h = pl.program_id(0)
    n_heads = pl.num_programs(0)
    c = lax.broadcasted_iota(jnp.int32, (MOBA_BLOCK, MOBA_BLOCK), 0)
    r = lax.broadcasted_iota(jnp.int32, (MOBA_BLOCK, MOBA_BLOCK), 1)
    max_exact = N_BUCKETS // 2

    def rb(b):
        return rb_ref[b * n_heads + h] * LOG2E

    for d in range(N_NEAR_BLOCKS):
        n = d * MOBA_BLOCK + r - c
        nn = jnp.maximum(n, 0)
        val = jnp.full((MOBA_BLOCK, MOBA_BLOCK), rb(0), F32)
        for b in range(1, max_exact):
            val = jnp.where(nn >= b, rb(b), val)
        for k, thr in enumerate(_BUCKET_THR):
            val = jnp.where(nn >= thr, rb(max_exact + k), val)
        if d == 0:
            val = jnp.where(n >= 0, val, NEG_INF)
        out_ref[0, d] = val
    out_ref[0, N_NEAR_BLOCKS] = jnp.zeros((MOBA_BLOCK, MOBA_BLOCK), F32)


def _bias_tiles(rel_bias):
    n_heads = rel_bias.shape[1]
    return pl.pallas_call(
        _bias_kernel,
        grid=(n_heads,),
        in_specs=[pl.BlockSpec(memory_space=pltpu.SMEM)],
        out_specs=pl.BlockSpec((1, N_NEAR_BLOCKS + 1, MOBA_BLOCK, MOBA_BLOCK), lambda h: (h, 0, 0, 0)),
        out_shape=jax.ShapeDtypeStruct((n_heads, N_NEAR_BLOCKS + 1, MOBA_BLOCK, MOBA_BLOCK), F32),
        name="bias_tiles",
    )(rel_bias.astype(F32).reshape(-1))


def _cast_weights_once(pairs):
    @pl.when(pl.program_id(0) == 0)
    def _():
        for w_ref, w_scr in pairs:
            for r in range(0, w_ref.shape[0], CAST_ROWS):
                w_scr[r:r + CAST_ROWS, :] = w_ref[r:r + CAST_ROWS, :].astype(BF16)


def _inproj_kernel(x_ref, g_ref, w_ref, cos_ref, sin_ref, qg_ref, kg_ref,
                   dec_ref, dq_ref, dk_ref, cd_ref, rg_ref,
                   proj_ref, kmean_ref, ret_ref, vt_ref, w_scr, rin_scr, kt_scr, state_ref,
                   *, seq_tiles):
    _cast_weights_once([(w_ref, w_scr)])
    x = x_ref[...]
    ms = jnp.mean(x * x, axis=-1, keepdims=True)
    h = (x * lax.rsqrt(ms + EPS) * g_ref[...]).astype(BF16)
    cos = cos_ref[...]
    sin = sin_ref[...]
    scale = HEAD_DIM ** -0.5
    rows = x.shape[0]
    heads = GROUP_WIDTH // HEAD_DIM

    def head_norm(y, gain):
        return y * lax.rsqrt(jnp.mean(y * y, axis=-1, keepdims=True) + EPS) * gain

    def section(s):
        return jnp.dot(h, w_scr[:, s * GROUP_WIDTH:(s + 1) * GROUP_WIDTH], preferred_element_type=F32)

    for s in range(RET_SECTIONS):
        y = section(s)
        for hh in range(heads):
            yh = y[:, hh * HEAD_DIM:(hh + 1) * HEAD_DIM]
            if s in (0, 1):
                yh = yh * cos + pltpu.roll(yh, HEAD_DIM // 2, axis=1) * sin
                if s == 1:
                    yh = yh * scale
            if s == 1:
                kt_scr[hh] = yh.T.astype(BF16)
            else:
                col = s * GROUP_WIDTH + hh * HEAD_DIM
                rin_scr[:, col:col + HEAD_DIM] = yh.astype(BF16)

    @pl.when(pl.program_id(0) % seq_tiles == 0)
    def _():
        state_ref[...] = jnp.zeros_like(state_ref)

    def moba_section(s):
        y = section(s)
        for hh in range(heads):
            yh = y[:, hh * HEAD_DIM:(hh + 1) * HEAD_DIM]
            if s == 4:
                yh = head_norm(yh, qg_ref[...]) * (scale * LOG2E)
            elif s == 5:
                yh = head_norm(yh, kg_ref[...])
                for blk in range(rows // MOBA_BLOCK):
                    kmean_ref[blk, :, hh * HEAD_DIM:(hh + 1) * HEAD_DIM] = jnp.mean(
                        yh[blk * MOBA_BLOCK:(blk + 1) * MOBA_BLOCK], axis=0, keepdims=True)
            if s == MOBA_V_SECTION:
                vt_ref[hh * HEAD_DIM:(hh + 1) * HEAD_DIM, :] = yh.T.astype(BF16)
            else:
                col = MOBA_PROJ_SECTIONS.index(s) * GROUP_WIDTH + hh * HEAD_DIM
                proj_ref[:, col:col + HEAD_DIM] = yh.astype(BF16)

    chunk = RET_CHUNK
    units = [(c, hh) for c in range(rows // chunk) for hh in range(RET_HEADS)]

    def operand(c, hh, sec):
        return rin_scr[c * chunk:(c + 1) * chunk,
                       sec * GROUP_WIDTH + hh * HEAD_DIM:sec * GROUP_WIDTH + (hh + 1) * HEAD_DIM]

    masked, kv = {}, {}
    for c, hh in units:
        q, v = operand(c, hh, 0), operand(c, hh, 2)
        k_t = kt_scr[hh, :, c * chunk:(c + 1) * chunk]
        sc = jnp.dot(q, k_t, preferred_element_type=F32) * dec_ref[hh]
        masked[c, hh] = sc.astype(BF16)
        v_dec = (v.astype(F32) * dk_ref[hh]).astype(BF16)
        kv[c, hh] = jnp.dot(k_t, v_dec, preferred_element_type=F32)
    before = {}
    for hh in range(RET_HEADS):
        state = state_ref[hh]
        for c in range(rows // chunk):
            before[c, hh] = state.astype(BF16)
            state = state * cd_ref[hh] + kv[c, hh]
        state_ref[hh] = state
    for c, hh in units:
        q, v = operand(c, hh, 0), operand(c, hh, 2)
        q_dec = (q.astype(F32) * dq_ref[hh]).astype(BF16)
        o = jnp.dot(jnp.concatenate([masked[c, hh], q_dec], axis=1),
                    jnp.concatenate([v, before[c, hh]], axis=0), preferred_element_type=F32)
        mu = jnp.mean(o, axis=-1, keepdims=True)
        oc = o - mu
        var = jnp.mean(oc * oc, axis=-1, keepdims=True)
        on = oc * lax.rsqrt(var + EPS) * rg_ref[:, hh * HEAD_DIM:(hh + 1) * HEAD_DIM]
        z = operand(c, hh, 3).astype(F32)
        ret_ref[c * chunk:(c + 1) * chunk, hh * HEAD_DIM:(hh + 1) * HEAD_DIM] = (
            z * jax.nn.sigmoid(z) * on).astype(BF16)

    for s in range(RET_SECTIONS, N_SECTIONS):
        moba_section(s)


def _retention_tables():
    c = RET_CHUNK
    log_g = np.log1p(-np.power(2.0, -5.0 - np.arange(RET_HEADS, dtype=np.float64)))
    j = np.arange(c, dtype=np.float64)
    rel = j[:, None] - j[None, :]
    decay = np.where(rel >= 0, np.exp(log_g[:, None, None] * np.maximum(rel, 0.0)[None]), 0.0)
    dq = np.exp(log_g[:, None] * (j + 1.0)[None])
    dk = np.exp(log_g[:, None] * (c - 1 - j)[None])
    cd = np.exp(log_g * c)
    wide = lambda t: np.broadcast_to(t[:, :, None], (RET_HEADS, c, HEAD_DIM))
    return (jnp.asarray(decay, F32), jnp.asarray(wide(dq), F32), jnp.asarray(wide(dk), F32),
            jnp.asarray(np.broadcast_to(cd[:, None, None], (RET_HEADS, 1, HEAD_DIM)), F32))


def _inproj(x2, layer, norm_g, w_in, cos, sin, q_g, k_g, ret_g, seq):
    m, d_model = x2.shape
    width = w_in.shape[2]
    moba_width = len(MOBA_PROJ_SECTIONS) * GROUP_WIDTH
    tm = ROW_TILE
    seq_tiles = seq // tm
    tables = _retention_tables()
    full = lambda a: pl.BlockSpec(a.shape, lambda i: (0,) * a.ndim)
    row1 = lambda n: pl.BlockSpec((None, 1, n), lambda i: (layer, 0, 0))
    return pl.pallas_call(
        functools.partial(_inproj_kernel, seq_tiles=seq_tiles),
        grid=(m // tm,),
        in_specs=[
            pl.BlockSpec((tm, d_model), lambda i: (i, 0)),
            row1(d_model),
            pl.BlockSpec((None, d_model, width), lambda i: (layer, 0, 0),
                         pipeline_mode=pl.Buffered(1)),
            pl.BlockSpec((tm, HEAD_DIM), lambda i: (i % seq_tiles, 0)),
            pl.BlockSpec((tm, HEAD_DIM), lambda i: (i % seq_tiles, 0)),
            row1(HEAD_DIM), row1(HEAD_DIM),
            *[full(t) for t in tables],
            row1(GROUP_WIDTH),
        ],
        out_specs=[
            pl.BlockSpec((tm, moba_width), lambda i: (i, 0)),
            pl.BlockSpec((tm // MOBA_BLOCK, 1, GROUP_WIDTH), lambda i: (i, 0, 0)),
            pl.BlockSpec((tm, GROUP_WIDTH), lambda i: (i, 0)),
            pl.BlockSpec((None, GROUP_WIDTH, tm), lambda i: (i, 0, 0)),
        ],
        out_shape=[
            jax.ShapeDtypeStruct((m, moba_width), BF16),
            jax.ShapeDtypeStruct((m // MOBA_BLOCK, 1, GROUP_WIDTH), F32),
            jax.ShapeDtypeStruct((m, GROUP_WIDTH), BF16),
            jax.ShapeDtypeStruct((m // tm, GROUP_WIDTH, tm), BF16),
        ],
        scratch_shapes=[pltpu.VMEM((d_model, width), BF16),
                        pltpu.VMEM((tm, RET_SECTIONS * GROUP_WIDTH), BF16),
                        pltpu.VMEM((RET_HEADS, HEAD_DIM, tm), BF16),
                        pltpu.VMEM((RET_HEADS, HEAD_DIM, HEAD_DIM), F32)],
        compiler_params=pltpu.CompilerParams(
            dimension_semantics=("arbitrary",), vmem_limit_bytes=VMEM_LIMIT),
        name="in_projection",
    )(x2, norm_g, w_in, cos, sin, q_g, k_g, *tables, ret_g)


def _moba_kernel(rb_ref, q_ref, k_ref, vt_ref, z_ref, km_ref, oh_ref, bias_ref, o_ref,
                 qaug_scr, m_scr, acc_scr, s_scr):
    blk = MOBA_BLOCK
    tile = MOBA_TILE
    sub = SUBLANES
    ti = pl.program_id(1)
    n_heads = MOBA_HEADS
    sel_rows = -(-(k_ref.shape[0] // blk) // BF16_ROWS) * BF16_ROWS
    blk_id = lax.broadcasted_iota(jnp.int32, (sel_rows, tile), 0)
    blk_f = blk_id.astype(F32)
    q_blk = 2 * ti + lax.broadcasted_iota(jnp.int32, (sel_rows, tile), 1) // blk
    unused = jnp.zeros((SEL_LANES - sel_rows, tile), BF16)
    past = blk_id < q_blk
    far = (q_blk - blk_id) >= N_NEAR_BLOCKS

    def key_rows(step):
        return pl.ds(pl.multiple_of((ti - step) * tile, tile), tile)

    def bias_tile(h, step):
        def part(offset):
            idx = jnp.where(offset < 0, N_NEAR_BLOCKS, jnp.minimum(offset, N_NEAR_BLOCKS))
            return bias_ref[h, idx]
        diag = part(2 * step)
        return jnp.concatenate([jnp.concatenate([diag, part(2 * step + 1)], axis=1),
                                jnp.concatenate([part(2 * step - 1), diag], axis=1)], axis=0)

    def scores(step, slot, h, with_bias):
        rows = key_rows(step)
        k_aug = jnp.concatenate([k_ref[rows, h * HEAD_DIM:(h + 1) * HEAD_DIM], oh_ref[rows, :]], axis=1)
        s = jnp.dot(k_aug, qaug_scr[h], preferred_element_type=F32)
        if with_bias:
            s = s + bias_tile(h, step)
        s_scr[slot, h] = s

    for h in range(n_heads):
        cols = slice(h * HEAD_DIM, (h + 1) * HEAD_DIM)
        q = q_ref[:, cols]
        km = km_ref[0, :sel_rows, cols]
        km_hi = km.astype(BF16)
        km_lo = (km - km_hi.astype(F32)).astype(BF16)
        gate = lax.dot_general(jnp.concatenate([km_hi, km_lo], axis=1),
                               jnp.concatenate([q, q], axis=1), _NT,
                               preferred_element_type=F32)
        g = jnp.where(past, gate, MASKED_GATE)
        sel = jnp.zeros(gate.shape, F32)
        for _ in range(MOBA_TOPK):
            top = jnp.max(g, axis=0, keepdims=True)
            first = jnp.min(jnp.where(g == top, blk_f, 1e9), axis=0, keepdims=True)
            pick = blk_f == first
            sel = jnp.where(pick, 1.0, sel)
            g = jnp.where(pick, MASKED_GATE, g)
        chosen = jnp.where(past, sel, 0.0) > 0.0
        far_c = jnp.full(gate.shape, rb_ref[(N_BUCKETS - 1) * n_heads + h] * LOG2E, F32)
        far_hi = far_c.astype(BF16).astype(F32)
        mask_hi = jnp.where(chosen, jnp.where(far, far_hi, 0.0), NEG_INF)
        mask_hi = jnp.where(blk_id == q_blk, 0.0, mask_hi)
        mask_lo = jnp.where(chosen, jnp.where(far, far_c - far_hi, 0.0), 0.0)
        q_t = q.astype(F32).T.astype(BF16)
        qaug_scr[h] = jnp.concatenate(
            [q_t, mask_hi.astype(BF16), unused, mask_lo.astype(BF16), unused], axis=0)

    m_scr[...] = jnp.full(m_scr.shape, NEG_INF, F32)
    ones_rows = jnp.ones((BF16_ROWS, tile), BF16)
    acc_scr[...] = jnp.zeros(acc_scr.shape, F32)

    def accumulate(step, slot, h):
        s = s_scr[slot, h].reshape(tile // sub, sub, tile)
        m_prev = m_scr[h]
        m_tile = jnp.max(jnp.max(s, axis=0), axis=0, keepdims=True)
        m_next = jnp.maximum(m_prev, jnp.broadcast_to(m_tile, m_prev.shape))
        alpha = jnp.exp2(m_prev - m_next)
        p = jnp.exp2(s - m_next[None])
        m_scr[h] = m_next
        vt_aug = jnp.concatenate([vt_ref[ti - step, h * HEAD_DIM:(h + 1) * HEAD_DIM, :], ones_rows],
                                 axis=0)
        pv = jnp.dot(vt_aug, p.reshape(tile, tile).astype(BF16),
                     preferred_element_type=F32)
        acc = acc_scr[h].reshape(ACC_ROWS // sub, sub, tile) * alpha[None]
        acc_scr[h] = acc.reshape(ACC_ROWS, tile) + pv

    def loop_body(next_with_bias):
        def one_step(step, slot):
            for h in range(n_heads):
                scores(step + 1, 1 - slot, h, next_with_bias)
                accumulate(step, slot, h)

        def body(step, carry):
            for slot in (0, 1):
                pl.when((step & 1) == slot)(functools.partial(one_step, step, slot))
            return carry
        return body

    n_biased = jnp.minimum(ti, MOBA_NEAR_STEPS - 1)
    for h in range(n_heads):
        scores(0, 0, h, True)
    lax.fori_loop(0, n_biased, loop_body(True), 0)
    lax.fori_loop(n_biased, ti, loop_body(False), 0)
    for h in range(n_heads):
        accumulate(ti, ti & 1, h)
        cols = slice(h * HEAD_DIM, (h + 1) * HEAD_DIM)
        z = z_ref[:, cols].astype(F32)
        inv = 1.0 / acc_scr[h, HEAD_DIM:HEAD_DIM + 1, :]
        out = (acc_scr[h, :HEAD_DIM, :] * inv).T
        o_ref[:, cols] = (z * jax.nn.sigmoid(z) * out).astype(BF16)


def _moba(proj, v_t, kmean, bias_tiles, rel_bias, batch, seq):
    m = proj.shape[0]
    tile = MOBA_TILE
    nq = seq // tile
    row_blk = np.arange(seq)[:, None] // MOBA_BLOCK
    onehot = jnp.asarray((np.arange(HEAD_DIM)[None, :] & (SEL_LANES - 1)) == row_blk, BF16)
    rows = lambda sec: pl.BlockSpec((tile, GROUP_WIDTH), lambda b, i: (b * nq + i, sec))
    whole = lambda sec: pl.BlockSpec((seq, GROUP_WIDTH), lambda b, i: (b, sec))
    return pl.pallas_call(
        _moba_kernel,
        grid=(batch, nq),
        in_specs=[pl.BlockSpec(memory_space=pltpu.SMEM),
                  rows(0), whole(1),
                  pl.BlockSpec((nq, GROUP_WIDTH, tile), lambda b, i: (b, 0, 0),
                               pipeline_mode=pl.Buffered(1)),
                  rows(2),
                  pl.BlockSpec((1, SEL_LANES, GROUP_WIDTH), lambda b, i: (b, 0, 0)),
                  pl.BlockSpec(onehot.shape, lambda b, i: (0, 0), pipeline_mode=pl.Buffered(1)),
                  pl.BlockSpec(bias_tiles.shape, lambda b, i: (0, 0, 0, 0),
                               pipeline_mode=pl.Buffered(1))],
        out_specs=pl.BlockSpec((tile, GROUP_WIDTH), lambda b, i: (b * nq + i, 0)),
        out_shape=jax.ShapeDtypeStruct((m, GROUP_WIDTH), BF16),
        scratch_shapes=[pltpu.VMEM((MOBA_HEADS, 2 * HEAD_DIM, tile), BF16),
                        pltpu.VMEM((MOBA_HEADS, SUBLANES, tile), F32),
                        pltpu.VMEM((MOBA_HEADS, ACC_ROWS, tile), F32),
                        pltpu.VMEM((2, MOBA_HEADS, tile, tile), F32)],
        compiler_params=pltpu.CompilerParams(
            dimension_semantics=("arbitrary", "arbitrary"), vmem_limit_bytes=VMEM_LIMIT),
        name="moba",
    )(rel_bias.astype(F32).reshape(-1), proj, proj, v_t, proj, kmean, onehot, bias_tiles)


def _outproj_kernel(x_ref, r_ref, a_ref, wo_ref, p_ref, wp_ref, pg_ref, wg_ref, o_ref,
                    wo_scr, wp_scr, wg_scr):
    _cast_weights_once([(wo_ref, wo_scr), (wp_ref, wp_scr), (wg_ref, wg_scr)])

    def unit_rms(t):
        return t * lax.rsqrt(jnp.mean(t * t, axis=-1, keepdims=True) + EPS)

    slab = x_ref.shape[0] // OUT_SLABS
    slabs = [slice(c * slab, (c + 1) * slab) for c in range(OUT_SLABS)]
    for rows in slabs:
        y = (jnp.dot(r_ref[rows, :], wo_scr[:GROUP_WIDTH, :], preferred_element_type=F32)
             + jnp.dot(a_ref[rows, :], wo_scr[GROUP_WIDTH:, :], preferred_element_type=F32))
        o_ref[rows, :] = x_ref[rows, :] + y
    for rows in slabs:
        e = unit_rms(jnp.dot(p_ref[rows, :].astype(BF16), wp_scr[...], preferred_element_type=F32))
        x1 = o_ref[rows, :]
        gate = jax.nn.sigmoid(jnp.dot(unit_rms(x1).astype(BF16), wg_scr[...],
                                      preferred_element_type=F32))
        o_ref[rows, :] = x1 + gate * (e * pg_ref[...])


def _outproj(x2, r_out, m_out, layer, w_out, p3, w_ple, ple_g, w_gate):
    m, d_model = x2.shape
    tm = OUT_ROW_TILE
    rows = lambda a: pl.BlockSpec((tm, a.shape[1]), lambda i: (i, 0))
    full = lambda a: pl.BlockSpec((None,) + a.shape[1:], lambda i: (layer, 0, 0),
                                  pipeline_mode=pl.Buffered(1))
    return pl.pallas_call(
        _outproj_kernel,
        grid=(m // tm,),
        in_specs=[rows(x2), rows(r_out), rows(m_out), full(w_out),
                  pl.BlockSpec((None, tm, p3.shape[2]), lambda i: (layer, i, 0)), full(w_ple),
                  full(ple_g), full(w_gate)],
        out_specs=pl.BlockSpec((tm, d_model), lambda i: (i, 0)),
        out_shape=jax.ShapeDtypeStruct((m, d_model), F32),
        scratch_shapes=[pltpu.VMEM(w.shape[1:], BF16) for w in (w_out, w_ple, w_gate)],
        compiler_params=pltpu.CompilerParams(
            dimension_semantics=("arbitrary",), vmem_limit_bytes=VMEM_LIMIT),
        name="out_projection",
    )(x2, r_out, m_out, w_out, p3, w_ple, ple_g, w_gate)


def _rotary_tables(seq):
    half = HEAD_DIM // 2
    inv = np.float32(ROPE_THETA) ** (-np.arange(half, dtype=np.float32) / np.float32(half))
    ang = np.arange(seq, dtype=np.float32)[:, None] * inv[None, :]
    cos, sin = np.cos(ang), np.sin(ang)
    return (jnp.asarray(np.concatenate([cos, cos], axis=1), F32),
            jnp.asarray(np.concatenate([-sin, sin], axis=1), F32))


def kernel(x, p, norm_g, w_in, ret_norm_g, q_norm_g, k_norm_g, rel_bias, w_out, w_ple, ple_norm_g, w_ple_gate):
    batch, seq, d_model = x.shape
    depth = w_in.shape[0]
    m = batch * seq
    assert w_in.shape[2] == N_SECTIONS * GROUP_WIDTH and w_out.shape[1] == 2 * GROUP_WIDTH
    assert rel_bias.shape == (N_BUCKETS, MOBA_HEADS)
    assert seq % ROW_TILE == 0 and ROW_TILE % MOBA_BLOCK == 0 and seq % MOBA_TILE == 0
    assert m % OUT_ROW_TILE == 0
    assert ROW_TILE == MOBA_TILE
    assert ROW_TILE % RET_CHUNK == 0 and seq // MOBA_BLOCK <= SEL_LANES

    cos, sin = _rotary_tables(seq)
    bias_tiles = _bias_tiles(rel_bias)
    x2 = x.reshape(m, d_model)
    nb = seq // MOBA_BLOCK
    stack = lambda g: g.reshape(depth, 1, -1)
    p3 = p.reshape(depth, m, -1)
    for i in range(depth):
        proj, kmean, r_out, v_t = _inproj(x2, i, stack(norm_g), w_in, cos, sin,
                                     stack(q_norm_g), stack(k_norm_g), stack(ret_norm_g), seq)
        kmean = jnp.pad(kmean.reshape(batch, nb, GROUP_WIDTH), ((0, 0), (0, SEL_LANES - nb), (0, 0)))
        m_out = _moba(proj, v_t, kmean, bias_tiles, rel_bias, batch, seq)
        x2 = _outproj(x2, r_out, m_out, i, w_out, p3, w_ple, stack(ple_norm_g), w_ple_gate)
    return x2.reshape(batch, seq, d_model)
```

```python
import functools
import math

import numpy as np
import jax
import jax.numpy as jnp
from jax import lax
from jax.experimental import pallas as pl
from jax.experimental.pallas import tpu as pltpu

F32 = jnp.float32
BF16 = jnp.bfloat16

HEAD_DIM = 128
SUBLANES = 8
BF16_ROWS = 16
ACC_ROWS = HEAD_DIM + BF16_ROWS
RET_HEADS = 4
MOBA_HEADS = 4
GROUP_WIDTH = 512
N_SECTIONS = 8
RET_SECTIONS = 4
MOBA_T_SECTIONS = (4, 6)
MOBA_PROJ_SECTIONS = (5, 7)
RET_CHUNK = 128
MOBA_BLOCK = 256
MOBA_TOPK = 3
N_BUCKETS = 32
MAX_DISTANCE = 2048
ROPE_THETA = 10000.0
EPS = 1e-6
NEG_INF = -1e30
MASKED_GATE = -3e38
LOG2E = math.log2(math.e)
SEL_LANES = HEAD_DIM // 2

ROW_TILE = 512
OUT_ROW_TILE = 1024
OUT_SLABS = 4
CAST_ROWS = 256
VMEM_LIMIT = 56 * 1024 * 1024


def _bucket_thresholds():
    max_exact = N_BUCKETS // 2
    thr = []
    for k in range(N_BUCKETS - max_exact):
        n = max_exact
        while min(max_exact + int(math.log(n / max_exact) / math.log(MAX_DISTANCE / max_exact)
                                  * (N_BUCKETS - max_exact)), N_BUCKETS - 1) < max_exact + k:
            n += 1
        thr.append(n)
    return thr


_BUCKET_THR = _bucket_thresholds()
N_NEAR_BLOCKS = next(d for d in range(1, 1 << 20) if (d - 1) * MOBA_BLOCK + 1 >= _BUCKET_THR[-1])
MOBA_TILE = 2 * MOBA_BLOCK
MOBA_NEAR_STEPS = N_NEAR_BLOCKS // 2 + 1


def _bias_kernel(rb_ref, out_ref):
    h = pl.program_id(0)
    n_heads = pl.num_programs(0)
    c = lax.broadcasted_iota(jnp.int32, (MOBA_BLOCK, MOBA_BLOCK), 0)
    r = lax.broadcasted_iota(jnp.int32, (MOBA_BLOCK, MOBA_BLOCK), 1)
    max_exact = N_BUCKETS // 2

    def rb(b):
        return rb_ref[b * n_heads + h] * LOG2E

    for d in range(N_NEAR_BLOCKS):
        n = d * MOBA_BLOCK + r - c
        nn = jnp.maximum(n, 0)
        val = jnp.full((MOBA_BLOCK, MOBA_BLOCK), rb(0), F32)
        for b in range(1, max_exact):
            val = jnp.where(nn >= b, rb(b), val)
        for k, thr in enumerate(_BUCKET_THR):
            val = jnp.where(nn >= thr, rb(max_exact + k), val)
        if d == 0:
            val = jnp.where(n >= 0, val, NEG_INF)
        out_ref[0, d] = val
    out_ref[0, N_NEAR_BLOCKS] = jnp.zeros((MOBA_BLOCK, MOBA_BLOCK), F32)


def _bias_tiles(rel_bias):
    n_heads = rel_bias.shape[1]
    return pl.pallas_call(
        _bias_kernel,
        grid=(n_heads,),
        in_specs=[pl.BlockSpec(memory_space=pltpu.SMEM)],
        out_specs=pl.BlockSpec((1, N_NEAR_BLOCKS + 1, MOBA_BLOCK, MOBA_BLOCK), lambda h: (h, 0, 0, 0)),
        out_shape=jax.ShapeDtypeStruct((n_heads, N_NEAR_BLOCKS + 1, MOBA_BLOCK, MOBA_BLOCK), F32),
        name="bias_tiles",
    )(rel_bias.astype(F32).reshape(-1))


def _cast_weights_once(pairs):
    @pl.when(pl.program_id(0) == 0)
    def _():
        for w_ref, w_scr in pairs:
            for r in range(0, w_ref.shape[0], CAST_ROWS):
                w_scr[r:r + CAST_ROWS, :] = w_ref[r:r + CAST_ROWS, :].astype(BF16)


def _inproj_kernel(x_ref, g_ref, w_ref, cos_ref, sin_ref, qg_ref, kg_ref,
                   dec_ref, dq_ref, dk_ref, cd_ref, rg_ref,
                   proj_ref, kmean_ref, ret_ref, qt_ref, vt_ref, w_scr, rin_scr, kt_scr, state_ref,
                   *, seq_tiles):
    _cast_weights_once([(w_ref, w_scr)])
    x = x_ref[...]
    ms = jnp.mean(x * x, axis=-1, keepdims=True)
    h = (x * lax.rsqrt(ms + EPS) * g_ref[...]).astype(BF16)
    cos = cos_ref[...]
    sin = sin_ref[...]
    scale = HEAD_DIM ** -0.5
    rows = x.shape[0]
    heads = GROUP_WIDTH // HEAD_DIM

    def head_norm(y, gain):
        return y * lax.rsqrt(jnp.mean(y * y, axis=-1, keepdims=True) + EPS) * gain

    def section(s):
        return jnp.dot(h, w_scr[:, s * GROUP_WIDTH:(s + 1) * GROUP_WIDTH], preferred_element_type=F32)

    for s in range(RET_SECTIONS):
        y = section(s)
        for hh in range(heads):
            yh = y[:, hh * HEAD_DIM:(hh + 1) * HEAD_DIM]
            if s in (0, 1):
                yh = yh * cos + pltpu.roll(yh, HEAD_DIM // 2, axis=1) * sin
                if s == 1:
                    yh = yh * scale
            if s == 1:
                kt_scr[hh] = yh.T.astype(BF16)
            else:
                col = s * GROUP_WIDTH + hh * HEAD_DIM
                rin_scr[:, col:col + HEAD_DIM] = yh.astype(BF16)

    @pl.when(pl.program_id(0) % seq_tiles == 0)
    def _():
        state_ref[...] = jnp.zeros_like(state_ref)

    def moba_section(s):
        y = section(s)
        for hh in range(heads):
            yh = y[:, hh * HEAD_DIM:(hh + 1) * HEAD_DIM]
            if s == 4:
                yh = head_norm(yh, qg_ref[...]) * (scale * LOG2E)
            elif s == 5:
                yh = head_norm(yh, kg_ref[...])
                for blk in range(rows // MOBA_BLOCK):
                    kmean_ref[blk, :, hh * HEAD_DIM:(hh + 1) * HEAD_DIM] = jnp.mean(
                        yh[blk * MOBA_BLOCK:(blk + 1) * MOBA_BLOCK], axis=0, keepdims=True)
            if s in MOBA_T_SECTIONS:
                t_ref = qt_ref if s == MOBA_T_SECTIONS[0] else vt_ref
                t_ref[hh * HEAD_DIM:(hh + 1) * HEAD_DIM, :] = yh.T.astype(BF16)
            else:
                col = MOBA_PROJ_SECTIONS.index(s) * GROUP_WIDTH + hh * HEAD_DIM
                proj_ref[:, col:col + HEAD_DIM] = yh.astype(BF16)

    chunk = RET_CHUNK
    units = [(c, hh) for c in range(rows // chunk) for hh in range(RET_HEADS)]

    def operand(c, hh, sec):
        return rin_scr[c * chunk:(c + 1) * chunk,
                       sec * GROUP_WIDTH + hh * HEAD_DIM:sec * GROUP_WIDTH + (hh + 1) * HEAD_DIM]

    masked, kv = {}, {}
    for c, hh in units:
        q, v = operand(c, hh, 0), operand(c, hh, 2)
        k_t = kt_scr[hh, :, c * chunk:(c + 1) * chunk]
        sc = jnp.dot(q, k_t, preferred_element_type=F32) * dec_ref[hh]
        masked[c, hh] = sc.astype(BF16)
        v_dec = (v.astype(F32) * dk_ref[hh]).astype(BF16)
        kv[c, hh] = jnp.dot(k_t, v_dec, preferred_element_type=F32)
    before = {}
    for hh in range(RET_HEADS):
        state = state_ref[hh]
        for c in range(rows // chunk):
            before[c, hh] = state.astype(BF16)
            state = state * cd_ref[hh] + kv[c, hh]
        state_ref[hh] = state
    for c, hh in units:
        q, v = operand(c, hh, 0), operand(c, hh, 2)
        q_dec = (q.astype(F32) * dq_ref[hh]).astype(BF16)
        o = jnp.dot(jnp.concatenate([masked[c, hh], q_dec], axis=1),
                    jnp.concatenate([v, before[c, hh]], axis=0), preferred_element_type=F32)
        mu = jnp.mean(o, axis=-1, keepdims=True)
        oc = o - mu
        var = jnp.mean(oc * oc, axis=-1, keepdims=True)
        on = oc * lax.rsqrt(var + EPS) * rg_ref[:, hh * HEAD_DIM:(hh + 1) * HEAD_DIM]
        z = operand(c, hh, 3).astype(F32)
        ret_ref[c * chunk:(c + 1) * chunk, hh * HEAD_DIM:(hh + 1) * HEAD_DIM] = (
            z * jax.nn.sigmoid(z) * on).astype(BF16)

    for s in range(RET_SECTIONS, N_SECTIONS):
        moba_section(s)


def _retention_tables():
    c = RET_CHUNK
    log_g = np.log1p(-np.power(2.0, -5.0 - np.arange(RET_HEADS, dtype=np.float64)))
    j = np.arange(c, dtype=np.float64)
    rel = j[:, None] - j[None, :]
    decay = np.where(rel >= 0, np.exp(log_g[:, None, None] * np.maximum(rel, 0.0)[None]), 0.0)
    dq = np.exp(log_g[:, None] * (j + 1.0)[None])
    dk = np.exp(log_g[:, None] * (c - 1 - j)[None])
    cd = np.exp(log_g * c)
    wide = lambda t: np.broadcast_to(t[:, :, None], (RET_HEADS, c, HEAD_DIM))
    return (jnp.asarray(decay, F32), jnp.asarray(wide(dq), F32), jnp.asarray(wide(dk), F32),
            jnp.asarray(np.broadcast_to(cd[:, None, None], (RET_HEADS, 1, HEAD_DIM)), F32))


def _inproj(x2, layer, norm_g, w_in, cos, sin, q_g, k_g, ret_g, seq):
    m, d_model = x2.shape
    width = w_in.shape[2]
    moba_width = len(MOBA_PROJ_SECTIONS) * GROUP_WIDTH
    tm = ROW_TILE
    seq_tiles = seq // tm
    tables = _retention_tables()
    full = lambda a: pl.BlockSpec(a.shape, lambda i: (0,) * a.ndim)
    row1 = lambda n: pl.BlockSpec((None, 1, n), lambda i: (layer, 0, 0))
    return pl.pallas_call(
        functools.partial(_inproj_kernel, seq_tiles=seq_tiles),
        grid=(m // tm,),
        in_specs=[
            pl.BlockSpec((tm, d_model), lambda i: (i, 0)),
            row1(d_model),
            pl.BlockSpec((None, d_model, width), lambda i: (layer, 0, 0),
                         pipeline_mode=pl.Buffered(1)),
            pl.BlockSpec((tm, HEAD_DIM), lambda i: (i % seq_tiles, 0)),
            pl.BlockSpec((tm, HEAD_DIM), lambda i: (i % seq_tiles, 0)),
            row1(HEAD_DIM), row1(HEAD_DIM),
            *[full(t) for t in tables],
            row1(GROUP_WIDTH),
        ],
        out_specs=[
            pl.BlockSpec((tm, moba_width), lambda i: (i, 0)),
            pl.BlockSpec((tm // MOBA_BLOCK, 1, GROUP_WIDTH), lambda i: (i, 0, 0)),
            pl.BlockSpec((tm, GROUP_WIDTH), lambda i: (i, 0)),
            pl.BlockSpec((None, GROUP_WIDTH, tm), lambda i: (i, 0, 0)),
            pl.BlockSpec((None, GROUP_WIDTH, tm), lambda i: (i, 0, 0)),
        ],
        out_shape=[
            jax.ShapeDtypeStruct((m, moba_width), BF16),
            jax.ShapeDtypeStruct((m // MOBA_BLOCK, 1, GROUP_WIDTH), F32),
            jax.ShapeDtypeStruct((m, GROUP_WIDTH), BF16),
            jax.ShapeDtypeStruct((m // tm, GROUP_WIDTH, tm), BF16),
            jax.ShapeDtypeStruct((m // tm, GROUP_WIDTH, tm), BF16),
        ],
        scratch_shapes=[pltpu.VMEM((d_model, width), BF16),
                        pltpu.VMEM((tm, RET_SECTIONS * GROUP_WIDTH), BF16),
                        pltpu.VMEM((RET_HEADS, HEAD_DIM, tm), BF16),
                        pltpu.VMEM((RET_HEADS, HEAD_DIM, HEAD_DIM), F32)],
        compiler_params=pltpu.CompilerParams(
            dimension_semantics=("arbitrary",), vmem_limit_bytes=VMEM_LIMIT),
        name="in_projection",
    )(x2, norm_g, w_in, cos, sin, q_g, k_g, *tables, ret_g)


def _moba_kernel(rb_ref, qt_ref, k_ref, vt_ref, z_ref, km_ref, oh_ref, bias_ref, o_ref,
                 qaug_scr, m_scr, acc_scr, s_scr, tmax_scr):
    blk = MOBA_BLOCK
    tile = MOBA_TILE
    sub = SUBLANES
    ti = pl.program_id(1)
    n_heads = MOBA_HEADS
    sel_rows = -(-(k_ref.shape[0] // blk) // BF16_ROWS) * BF16_ROWS
    blk_id = lax.broadcasted_iota(jnp.int32, (sel_rows, tile), 0)
    blk_f = blk_id.astype(F32)
    q_blk = 2 * ti + lax.broadcasted_iota(jnp.int32, (sel_rows, tile), 1) // blk
    unused = jnp.zeros((SEL_LANES - sel_rows, tile), BF16)
    past = blk_id < q_blk
    far = (q_blk - blk_id) >= N_NEAR_BLOCKS

    def key_rows(step):
        return pl.ds(pl.multiple_of((ti - step) * tile, tile), tile)

    def bias_tile(h, step):
        def part(offset):
            idx = jnp.where(offset < 0, N_NEAR_BLOCKS, jnp.minimum(offset, N_NEAR_BLOCKS))
            return bias_ref[h, idx]
        diag = part(2 * step)
        return jnp.concatenate([jnp.concatenate([diag, part(2 * step + 1)], axis=1),
                                jnp.concatenate([part(2 * step - 1), diag], axis=1)], axis=0)

    def scores(step, slot, h, with_bias):
        rows = key_rows(step)
        k_aug = jnp.concatenate([k_ref[rows, h * HEAD_DIM:(h + 1) * HEAD_DIM], oh_ref[rows, :]], axis=1)
        s = jnp.dot(k_aug, qaug_scr[h], preferred_element_type=F32)
        if with_bias:
            s = s + bias_tile(h, step)
        s_scr[slot, h] = s
        tmax_scr[slot, h] = jnp.max(s.reshape(tile // sub, sub, tile), axis=0)

    for h in range(n_heads):
        cols = slice(h * HEAD_DIM, (h + 1) * HEAD_DIM)
        q_t = qt_ref[cols, :]
        km = km_ref[0, :sel_rows, cols]
        km_hi = km.astype(BF16)
        km_lo = (km - km_hi.astype(F32)).astype(BF16)
        gate = jnp.dot(jnp.concatenate([km_hi, km_lo], axis=1), jnp.concatenate([q_t, q_t], axis=0),
                       preferred_element_type=F32)
        g = jnp.where(past, gate, MASKED_GATE)
        sel = jnp.zeros(gate.shape, F32)
        for _ in range(MOBA_TOPK):
            top = jnp.max(g, axis=0, keepdims=True)
            first = jnp.min(jnp.where(g == top, blk_f, 1e9), axis=0, keepdims=True)
            pick = blk_f == first
            sel = jnp.where(pick, 1.0, sel)
            g = jnp.where(pick, MASKED_GATE, g)
        chosen = jnp.where(past, sel, 0.0) > 0.0
        far_c = jnp.full(gate.shape, rb_ref[(N_BUCKETS - 1) * n_heads + h] * LOG2E, F32)
        far_hi = far_c.astype(BF16).astype(F32)
        mask_hi = jnp.where(chosen, jnp.where(far, far_hi, 0.0), NEG_INF)
        mask_hi = jnp.where(blk_id == q_blk, 0.0, mask_hi)
        mask_lo = jnp.where(chosen, jnp.where(far, far_c - far_hi, 0.0), 0.0)
        qaug_scr[h] = jnp.concatenate(
            [q_t, mask_hi.astype(BF16), unused, mask_lo.astype(BF16), unused], axis=0)

    m_scr[...] = jnp.full(m_scr.shape, NEG_INF, F32)
    ones_rows = jnp.ones((BF16_ROWS, tile), BF16)
    acc_scr[...] = jnp.zeros(acc_scr.shape, F32)

    def accumulate(step, slot, h):
        s = s_scr[slot, h].reshape(tile // sub, sub, tile)
        m_prev = m_scr[h]
        m_tile = jnp.max(tmax_scr[slot, h], axis=0, keepdims=True)
        m_next = jnp.maximum(m_prev, jnp.broadcast_to(m_tile, m_prev.shape))
        alpha = jnp.exp2(m_prev - m_next)
        p = jnp.exp2(s - m_next[None])
        m_scr[h] = m_next
        vt_aug = jnp.concatenate([vt_ref[ti - step, h * HEAD_DIM:(h + 1) * HEAD_DIM, :], ones_rows],
                                 axis=0)
        pv = jnp.dot(vt_aug, p.reshape(tile, tile).astype(BF16),
                     preferred_element_type=F32)
        acc = acc_scr[h].reshape(ACC_ROWS // sub, sub, tile) * alpha[None]
        acc_scr[h] = acc.reshape(ACC_ROWS, tile) + pv

    def loop_body(next_with_bias):
        def one_step(step, slot):
            for h in range(n_heads):
                scores(step + 1, 1 - slot, h, next_with_bias)
                accumulate(step, slot, h)

        def body(step, carry):
            for slot in (0, 1):
                pl.when((step & 1) == slot)(functools.partial(one_step, step, slot))
            return carry
        return body

    n_biased = jnp.minimum(ti, MOBA_NEAR_STEPS - 1)
    for h in range(n_heads):
        scores(0, 0, h, True)
    lax.fori_loop(0, n_biased, loop_body(True), 0)
    lax.fori_loop(n_biased, ti, loop_body(False), 0)
    for h in range(n_heads):
        accumulate(ti, ti & 1, h)
        cols = slice(h * HEAD_DIM, (h + 1) * HEAD_DIM)
        z = z_ref[:, cols].astype(F32)
        inv = 1.0 / acc_scr[h, HEAD_DIM:HEAD_DIM + 1, :]
        out = (acc_scr[h, :HEAD_DIM, :] * inv).T
        o_ref[:, cols] = (z * jax.nn.sigmoid(z) * out).astype(BF16)


def _moba(proj, q_t, v_t, kmean, bias_tiles, rel_bias, batch, seq):
    m = proj.shape[0]
    tile = MOBA_TILE
    nq = seq // tile
    row_blk = np.arange(seq)[:, None] // MOBA_BLOCK
    onehot = jnp.asarray((np.arange(HEAD_DIM)[None, :] & (SEL_LANES - 1)) == row_blk, BF16)
    rows = lambda sec: pl.BlockSpec((tile, GROUP_WIDTH), lambda b, i: (b * nq + i, sec))
    whole = lambda sec: pl.BlockSpec((seq, GROUP_WIDTH), lambda b, i: (b, sec))
    return pl.pallas_call(
        _moba_kernel,
        grid=(batch, nq),
        in_specs=[pl.BlockSpec(memory_space=pltpu.SMEM),
                  pl.BlockSpec((None, GROUP_WIDTH, tile), lambda b, i: (b * nq + i, 0, 0)),
                  whole(0),
                  pl.BlockSpec((nq, GROUP_WIDTH, tile), lambda b, i: (b, 0, 0),
                               pipeline_mode=pl.Buffered(1)),
                  rows(1),
                  pl.BlockSpec((1, SEL_LANES, GROUP_WIDTH), lambda b, i: (b, 0, 0)),
                  pl.BlockSpec(onehot.shape, lambda b, i: (0, 0), pipeline_mode=pl.Buffered(1)),
                  pl.BlockSpec(bias_tiles.shape, lambda b, i: (0, 0, 0, 0),
                               pipeline_mode=pl.Buffered(1))],
        out_specs=pl.BlockSpec((tile, GROUP_WIDTH), lambda b, i: (b * nq + i, 0)),
        out_shape=jax.ShapeDtypeStruct((m, GROUP_WIDTH), BF16),
        scratch_shapes=[pltpu.VMEM((MOBA_HEADS, 2 * HEAD_DIM, tile), BF16),
                        pltpu.VMEM((MOBA_HEADS, SUBLANES, tile), F32),
                        pltpu.VMEM((MOBA_HEADS, ACC_ROWS, tile), F32),
                        pltpu.VMEM((2, MOBA_HEADS, tile, tile), F32),
                        pltpu.VMEM((2, MOBA_HEADS, SUBLANES, tile), F32)],
        compiler_params=pltpu.CompilerParams(
            dimension_semantics=("arbitrary", "arbitrary"), vmem_limit_bytes=VMEM_LIMIT),
        name="moba",
    )(rel_bias.astype(F32).reshape(-1), q_t, proj, v_t, proj, kmean, onehot, bias_tiles)


def _outproj_kernel(x_ref, r_ref, a_ref, wo_ref, p_ref, wp_ref, pg_ref, wg_ref, o_ref,
                    wo_scr, wp_scr, wg_scr):
    _cast_weights_once([(wo_ref, wo_scr), (wp_ref, wp_scr), (wg_ref, wg_scr)])

    def unit_rms(t):
        return t * lax.rsqrt(jnp.mean(t * t, axis=-1, keepdims=True) + EPS)

    slab = x_ref.shape[0] // OUT_SLABS
    slabs = [slice(c * slab, (c + 1) * slab) for c in range(OUT_SLABS)]
    for rows in slabs:
        y = (jnp.dot(r_ref[rows, :], wo_scr[:GROUP_WIDTH, :], preferred_element_type=F32)
             + jnp.dot(a_ref[rows, :], wo_scr[GROUP_WIDTH:, :], preferred_element_type=F32))
        o_ref[rows, :] = x_ref[rows, :] + y
    for rows in slabs:
        e = unit_rms(jnp.dot(p_ref[rows, :].astype(BF16), wp_scr[...], preferred_element_type=F32))
        x1 = o_ref[rows, :]
        gate = jax.nn.sigmoid(jnp.dot(unit_rms(x1).astype(BF16), wg_scr[...],
                                      preferred_element_type=F32))
        o_ref[rows, :] = x1 + gate * (e * pg_ref[...])


def _outproj(x2, r_out, m_out, layer, w_out, p3, w_ple, ple_g, w_gate):
    m, d_model = x2.shape
    tm = OUT_ROW_TILE
    rows = lambda a: pl.BlockSpec((tm, a.shape[1]), lambda i: (i, 0))
    full = lambda a: pl.BlockSpec((None,) + a.shape[1:], lambda i: (layer, 0, 0),
                                  pipeline_mode=pl.Buffered(1))
    return pl.pallas_call(
        _outproj_kernel,
        grid=(m // tm,),
        in_specs=[rows(x2), rows(r_out), rows(m_out), full(w_out),
                  pl.BlockSpec((None, tm, p3.shape[2]), lambda i: (layer, i, 0)), full(w_ple),
                  full(ple_g), full(w_gate)],
        out_specs=pl.BlockSpec((tm, d_model), lambda i: (i, 0)),
        out_shape=jax.ShapeDtypeStruct((m, d_model), F32),
        scratch_shapes=[pltpu.VMEM(w.shape[1:], BF16) for w in (w_out, w_ple, w_gate)],
        compiler_params=pltpu.CompilerParams(
            dimension_semantics=("arbitrary",), vmem_limit_bytes=VMEM_LIMIT),
        name="out_projection",
    )(x2, r_out, m_out, w_out, p3, w_ple, ple_g, w_gate)


def _rotary_tables(seq):
    half = HEAD_DIM // 2
    inv = np.float32(ROPE_THETA) ** (-np.arange(half, dtype=np.float32) / np.float32(half))
    ang = np.arange(seq, dtype=np.float32)[:, None] * inv[None, :]
    cos, sin = np.cos(ang), np.sin(ang)
    return (jnp.asarray(np.concatenate([cos, cos], axis=1), F32),
            jnp.asarray(np.concatenate([-sin, sin], axis=1), F32))


def kernel(x, p, norm_g, w_in, ret_norm_g, q_norm_g, k_norm_g, rel_bias, w_out, w_ple, ple_norm_g, w_ple_gate):
    batch, seq, d_model = x.shape
    depth = w_in.shape[0]
    m = batch * seq
    assert w_in.shape[2] == N_SECTIONS * GROUP_WIDTH and w_out.shape[1] == 2 * GROUP_WIDTH
    assert rel_bias.shape == (N_BUCKETS, MOBA_HEADS)
    assert seq % ROW_TILE == 0 and ROW_TILE % MOBA_BLOCK == 0 and seq % MOBA_TILE == 0
    assert m % OUT_ROW_TILE == 0
    assert ROW_TILE == MOBA_TILE
    assert ROW_TILE % RET_CHUNK == 0 and seq // MOBA_BLOCK <= SEL_LANES

    cos, sin = _rotary_tables(seq)
    bias_tiles = _bias_tiles(rel_bias)
    x2 = x.reshape(m, d_model)
    nb = seq // MOBA_BLOCK
    stack = lambda g: g.reshape(depth, 1, -1)
    p3 = p.reshape(depth, m, -1)
    for i in range(depth):
        proj, kmean, r_out, q_t, v_t = _inproj(x2, i, stack(norm_g), w_in, cos, sin,
                                     stack(q_norm_g), stack(k_norm_g), stack(ret_norm_g), seq)
        kmean = jnp.pad(kmean.reshape(batch, nb, GROUP_WIDTH), ((0, 0), (0, SEL_LANES - nb), (0, 0)))
        m_out = _moba(proj, q_t, v_t, kmean, bias_tiles, rel_bias, batch, seq)
        x2 = _outproj(x2, r_out, m_out, i, w_out, p3, w_ple, stack(ple_norm_g), w_ple_gate)
    return x2.reshape(batch, seq, d_model)
```

```python
import functools
import math

import numpy as np
import jax
import jax.numpy as jnp
from jax import lax
from jax.experimental import pallas as pl
from jax.experimental.pallas import tpu as pltpu

F32 = jnp.float32
BF16 = jnp.bfloat16

HEAD_DIM = 128
SUBLANES = 8
BF16_ROWS = 16
ACC_ROWS = HEAD_DIM + BF16_ROWS
RET_HEADS = 4
MOBA_HEADS = 4
GROUP_WIDTH = 512
N_SECTIONS = 8
RET_SECTIONS = 4
MOBA_T_SECTIONS = (4, 6)
MOBA_PROJ_SECTIONS = (5, 7)
RET_CHUNK = 128
MOBA_BLOCK = 256
MOBA_TOPK = 3
N_BUCKETS = 32
MAX_DISTANCE = 2048
ROPE_THETA = 10000.0
EPS = 1e-6
NEG_INF = -1e30
MASKED_GATE = -3e38
LOG2E = math.log2(math.e)
SEL_LANES = HEAD_DIM // 2

ROW_TILE = 512
OUT_ROW_TILE = 1024
OUT_SLABS = 4
CAST_ROWS = 256
VMEM_LIMIT = 56 * 1024 * 1024


def _bucket_thresholds():
    max_exact = N_BUCKETS // 2
    thr = []
    for k in range(N_BUCKETS - max_exact):
        n = max_exact
        while min(max_exact + int(math.log(n / max_exact) / math.log(MAX_DISTANCE / max_exact)
                                  * (N_BUCKETS - max_exact)), N_BUCKETS - 1) < max_exact + k:
            n += 1
        thr.append(n)
    return thr


_BUCKET_THR = _bucket_thresholds()
N_NEAR_BLOCKS = next(d for d in range(1, 1 << 20) if (d - 1) * MOBA_BLOCK + 1 >= _BUCKET_THR[-1])
MOBA_TILE = 2 * MOBA_BLOCK
MOBA_NEAR_STEPS = N_NEAR_BLOCKS // 2 + 1


def _bias_kernel(rb_ref, out_ref):
    h = pl.program_id(0)
    n_heads = pl.num_programs(0)
    c = lax.broadcasted_iota(jnp.int32, (MOBA_BLOCK, MOBA_BLOCK), 0)
    r = lax.broadcasted_iota(jnp.int32, (MOBA_BLOCK, MOBA_BLOCK), 1)
    max_exact = N_BUCKETS // 2

    def rb(b):
        return rb_ref[b * n_heads + h] * LOG2E

    for d in range(N_NEAR_BLOCKS):
        n = d * MOBA_BLOCK + r - c
        nn = jnp.maximum(n, 0)
        val = jnp.full((MOBA_BLOCK, MOBA_BLOCK), rb(0), F32)
        for b in range(1, max_exact):
            val = jnp.where(nn >= b, rb(b), val)
        for k, thr in enumerate(_BUCKET_THR):
            val = jnp.where(nn >= thr, rb(max_exact + k), val)
        if d == 0:
            val = jnp.where(n >= 0, val, NEG_INF)
        out_ref[0, d] = val
    out_ref[0, N_NEAR_BLOCKS] = jnp.zeros((MOBA_BLOCK, MOBA_BLOCK), F32)


def _bias_tiles(rel_bias):
    n_heads = rel_bias.shape[1]
    return pl.pallas_call(
        _bias_kernel,
        grid=(n_heads,),
        in_specs=[pl.BlockSpec(memory_space=pltpu.SMEM)],
        out_specs=pl.BlockSpec((1, N_NEAR_BLOCKS + 1, MOBA_BLOCK, MOBA_BLOCK), lambda h: (h, 0, 0, 0)),
        out_shape=jax.ShapeDtypeStruct((n_heads, N_NEAR_BLOCKS + 1, MOBA_BLOCK, MOBA_BLOCK), F32),
        name="bias_tiles",
    )(rel_bias.astype(F32).reshape(-1))


def _cast_weights_once(pairs):
    @pl.when(pl.program_id(0) == 0)
    def _():
        for w_ref, w_scr in pairs:
            for r in range(0, w_ref.shape[0], CAST_ROWS):
                w_scr[r:r + CAST_ROWS, :] = w_ref[r:r + CAST_ROWS, :].astype(BF16)


def _inproj_kernel(x_ref, g_ref, w_ref, cos_ref, sin_ref, qg_ref, kg_ref,
                   dec_ref, dq_ref, dk_ref, cd_ref, rg_ref,
                   proj_ref, kmean_ref, ret_ref, qt_ref, vt_ref, w_scr, rin_scr, kt_scr, state_ref,
                   *, seq_tiles):
    _cast_weights_once([(w_ref, w_scr)])
    x = x_ref[...]
    ms = jnp.mean(x * x, axis=-1, keepdims=True)
    h = (x * lax.rsqrt(ms + EPS) * g_ref[...]).astype(BF16)
    cos = cos_ref[...]
    sin = sin_ref[...]
    scale = HEAD_DIM ** -0.5
    rows = x.shape[0]
    heads = GROUP_WIDTH // HEAD_DIM

    def head_norm(y, gain):
        return y * lax.rsqrt(jnp.mean(y * y, axis=-1, keepdims=True) + EPS) * gain

    def section(s):
        return jnp.dot(h, w_scr[:, s * GROUP_WIDTH:(s + 1) * GROUP_WIDTH], preferred_element_type=F32)

    for s in range(RET_SECTIONS):
        y = section(s)
        for hh in range(heads):
            yh = y[:, hh * HEAD_DIM:(hh + 1) * HEAD_DIM]
            if s in (0, 1):
                yh = yh * cos + pltpu.roll(yh, HEAD_DIM // 2, axis=1) * sin
                if s == 1:
                    yh = yh * scale
            if s == 1:
                kt_scr[hh] = yh.T.astype(BF16)
            else:
                col = s * GROUP_WIDTH + hh * HEAD_DIM
                rin_scr[:, col:col + HEAD_DIM] = yh.astype(BF16)

    @pl.when(pl.program_id(0) % seq_tiles == 0)
    def _():
        state_ref[...] = jnp.zeros_like(state_ref)

    def moba_section(s):
        y = section(s)
        for hh in range(heads):
            yh = y[:, hh * HEAD_DIM:(hh + 1) * HEAD_DIM]
            if s == 4:
                yh = head_norm(yh, qg_ref[...]) * (scale * LOG2E)
            elif s == 5:
                yh = head_norm(yh, kg_ref[...])
                for blk in range(rows // MOBA_BLOCK):
                    kmean_ref[blk, :, hh * HEAD_DIM:(hh + 1) * HEAD_DIM] = jnp.mean(
                        yh[blk * MOBA_BLOCK:(blk + 1) * MOBA_BLOCK], axis=0, keepdims=True)
            if s in MOBA_T_SECTIONS:
                t_ref = qt_ref if s == MOBA_T_SECTIONS[0] else vt_ref
                t_ref[hh * HEAD_DIM:(hh + 1) * HEAD_DIM, :] = yh.T.astype(BF16)
            else:
                col = MOBA_PROJ_SECTIONS.index(s) * GROUP_WIDTH + hh * HEAD_DIM
                proj_ref[:, col:col + HEAD_DIM] = yh.astype(BF16)

    chunk = RET_CHUNK
    units = [(c, hh) for c in range(rows // chunk) for hh in range(RET_HEADS)]

    def operand(c, hh, sec):
        return rin_scr[c * chunk:(c + 1) * chunk,
                       sec * GROUP_WIDTH + hh * HEAD_DIM:sec * GROUP_WIDTH + (hh + 1) * HEAD_DIM]

    masked, kv = {}, {}
    for c, hh in units:
        q, v = operand(c, hh, 0), operand(c, hh, 2)
        k_t = kt_scr[hh, :, c * chunk:(c + 1) * chunk]
        sc = jnp.dot(q, k_t, preferred_element_type=F32) * dec_ref[hh]
        masked[c, hh] = sc.astype(BF16)
        v_dec = (v.astype(F32) * dk_ref[hh]).astype(BF16)
        kv[c, hh] = jnp.dot(k_t, v_dec, preferred_element_type=F32)
    before = {}
    for hh in range(RET_HEADS):
        state = state_ref[hh]
        for c in range(rows // chunk):
            before[c, hh] = state.astype(BF16)
            state = state * cd_ref[hh] + kv[c, hh]
        state_ref[hh] = state
    for c, hh in units:
        q, v = operand(c, hh, 0), operand(c, hh, 2)
        q_dec = (q.astype(F32) * dq_ref[hh]).astype(BF16)
        o = jnp.dot(jnp.concatenate([masked[c, hh], q_dec], axis=1),
                    jnp.concatenate([v, before[c, hh]], axis=0), preferred_element_type=F32)
        mu = jnp.mean(o, axis=-1, keepdims=True)
        oc = o - mu
        var = jnp.mean(oc * oc, axis=-1, keepdims=True)
        on = oc * lax.rsqrt(var + EPS) * rg_ref[:, hh * HEAD_DIM:(hh + 1) * HEAD_DIM]
        z = operand(c, hh, 3).astype(F32)
        ret_ref[c * chunk:(c + 1) * chunk, hh * HEAD_DIM:(hh + 1) * HEAD_DIM] = (
            z * jax.nn.sigmoid(z) * on).astype(BF16)

    for s in range(RET_SECTIONS, N_SECTIONS):
        moba_section(s)


def _retention_tables():
    c = RET_CHUNK
    log_g = np.log1p(-np.power(2.0, -5.0 - np.arange(RET_HEADS, dtype=np.float64)))
    j = np.arange(c, dtype=np.float64)
    rel = j[:, None] - j[None, :]
    decay = np.where(rel >= 0, np.exp(log_g[:, None, None] * np.maximum(rel, 0.0)[None]), 0.0)
    dq = np.exp(log_g[:, None] * (j + 1.0)[None])
    dk = np.exp(log_g[:, None] * (c - 1 - j)[None])
    cd = np.exp(log_g * c)
    wide = lambda t: np.broadcast_to(t[:, :, None], (RET_HEADS, c, HEAD_DIM))
    return (jnp.asarray(decay, F32), jnp.asarray(wide(dq), F32), jnp.asarray(wide(dk), F32),
            jnp.asarray(np.broadcast_to(cd[:, None, None], (RET_HEADS, 1, HEAD_DIM)), F32))


def _inproj(x2, layer, norm_g, w_in, cos, sin, q_g, k_g, ret_g, seq):
    m, d_model = x2.shape
    width = w_in.shape[2]
    moba_width = len(MOBA_PROJ_SECTIONS) * GROUP_WIDTH
    tm = ROW_TILE
    seq_tiles = seq // tm
    tables = _retention_tables()
    full = lambda a: pl.BlockSpec(a.shape, lambda i: (0,) * a.ndim)
    row1 = lambda n: pl.BlockSpec((None, 1, n), lambda i: (layer, 0, 0))
    return pl.pallas_call(
        functools.partial(_inproj_kernel, seq_tiles=seq_tiles),
        grid=(m // tm,),
        in_specs=[
            pl.BlockSpec((tm, d_model), lambda i: (i, 0)),
            row1(d_model),
            pl.BlockSpec((None, d_model, width), lambda i: (layer, 0, 0),
                         pipeline_mode=pl.Buffered(1)),
            pl.BlockSpec((tm, HEAD_DIM), lambda i: (i % seq_tiles, 0)),
            pl.BlockSpec((tm, HEAD_DIM), lambda i: (i % seq_tiles, 0)),
            row1(HEAD_DIM), row1(HEAD_DIM),
            *[full(t) for t in tables],
            row1(GROUP_WIDTH),
        ],
        out_specs=[
            pl.BlockSpec((tm, moba_width), lambda i: (i, 0)),
            pl.BlockSpec((tm // MOBA_BLOCK, 1, GROUP_WIDTH), lambda i: (i, 0, 0)),
            pl.BlockSpec((tm, GROUP_WIDTH), lambda i: (i, 0)),
            pl.BlockSpec((None, GROUP_WIDTH, tm), lambda i: (i, 0, 0)),
            pl.BlockSpec((None, GROUP_WIDTH, tm), lambda i: (i, 0, 0)),
        ],
        out_shape=[
            jax.ShapeDtypeStruct((m, moba_width), BF16),
            jax.ShapeDtypeStruct((m // MOBA_BLOCK, 1, GROUP_WIDTH), F32),
            jax.ShapeDtypeStruct((m, GROUP_WIDTH), BF16),
            jax.ShapeDtypeStruct((m // tm, GROUP_WIDTH, tm), BF16),
            jax.ShapeDtypeStruct((m // tm, GROUP_WIDTH, tm), BF16),
        ],
        scratch_shapes=[pltpu.VMEM((d_model, width), BF16),
                        pltpu.VMEM((tm, RET_SECTIONS * GROUP_WIDTH), BF16),
                        pltpu.VMEM((RET_HEADS, HEAD_DIM, tm), BF16),
                        pltpu.VMEM((RET_HEADS, HEAD_DIM, HEAD_DIM), F32)],
        compiler_params=pltpu.CompilerParams(
            dimension_semantics=("arbitrary",), vmem_limit_bytes=VMEM_LIMIT),
        name="in_projection",
    )(x2, norm_g, w_in, cos, sin, q_g, k_g, *tables, ret_g)


def _moba_kernel(rb_ref, qt_ref, k_ref, vt_ref, z_ref, km_ref, oh_ref, bias_ref, o_ref,
                 qaug_scr, m_scr, acc_scr, s_scr, tmax_scr):
    blk = MOBA_BLOCK
    tile = MOBA_TILE
    sub = SUBLANES
    ti = pl.program_id(1)
    n_heads = MOBA_HEADS
    sel_rows = -(-(k_ref.shape[0] // blk) // BF16_ROWS) * BF16_ROWS
    blk_id = lax.broadcasted_iota(jnp.int32, (sel_rows, tile), 0)
    blk_f = blk_id.astype(F32)
    q_blk = 2 * ti + lax.broadcasted_iota(jnp.int32, (sel_rows, tile), 1) // blk
    unused = jnp.zeros((SEL_LANES - sel_rows, tile), BF16)
    past = blk_id < q_blk
    far = (q_blk - blk_id) >= N_NEAR_BLOCKS
    own = blk_id == q_blk
    spread = lambda a: jnp.broadcast_to(a[None], (sel_rows // sub,) + a.shape).reshape(sel_rows, tile)

    def key_rows(step):
        return pl.ds(pl.multiple_of((ti - step) * tile, tile), tile)

    def bias_tile(h, step):
        def part(offset):
            idx = jnp.where(offset < 0, N_NEAR_BLOCKS, jnp.minimum(offset, N_NEAR_BLOCKS))
            return bias_ref[h, idx]
        diag = part(2 * step)
        return jnp.concatenate([jnp.concatenate([diag, part(2 * step + 1)], axis=1),
                                jnp.concatenate([part(2 * step - 1), diag], axis=1)], axis=0)

    def key_tile(step, h):
        rows = key_rows(step)
        return jnp.concatenate([k_ref[rows, h * HEAD_DIM:(h + 1) * HEAD_DIM], oh_ref[rows, :]], axis=1)

    def scores(step, slot, h, with_bias):
        s = jnp.dot(key_tile(step, h), qaug_scr[h], preferred_element_type=F32)
        if with_bias:
            s = s + bias_tile(h, step)
        s_scr[slot, h] = s
        tmax_scr[slot, h] = jnp.max(s.reshape(tile // sub, sub, tile), axis=0)

    def own_scores(h):
        k_aug = key_tile(0, h)
        diag = bias_ref[h, 0]
        first = jnp.dot(k_aug[:blk], qaug_scr[h], preferred_element_type=F32)
        first = first + jnp.concatenate([diag, bias_ref[h, 1]], axis=1)
        second = jnp.dot(k_aug[blk:], qaug_scr[h, :, blk:], preferred_element_type=F32) + diag
        s_scr[0, h, :blk, :] = first
        s_scr[0, h, blk:, blk:] = second
        t_first = jnp.max(first.reshape(blk // sub, sub, tile), axis=0)
        t_second = jnp.max(second.reshape(blk // sub, sub, blk), axis=0)
        tmax_scr[0, h] = jnp.concatenate(
            [t_first[:, :blk], jnp.maximum(t_first[:, blk:], t_second)], axis=1)

    s_scr[0, :, blk:, :blk] = jnp.full((n_heads, blk, blk), NEG_INF, F32)

    for h in range(n_heads):
        cols = slice(h * HEAD_DIM, (h + 1) * HEAD_DIM)
        q_t = qt_ref[cols, :]
        km = km_ref[0, :sel_rows, cols]
        km_hi = km.astype(BF16)
        km_lo = (km - km_hi.astype(F32)).astype(BF16)
        gate = jnp.dot(jnp.concatenate([km_hi, km_lo], axis=1), jnp.concatenate([q_t, q_t], axis=0),
                       preferred_element_type=F32)
        g = jnp.where(past, gate, MASKED_GATE)
        for _ in range(MOBA_TOPK):
            top = jnp.max(g, axis=0, keepdims=True)
            first = jnp.min(jnp.where(g == top, blk_f, 1e9), axis=0, keepdims=True)
            g = jnp.where(blk_f == first, MASKED_GATE, g)
        take = (past & (g == MASKED_GATE)) | own
        far_c = jnp.full((sub, tile), rb_ref[(N_BUCKETS - 1) * n_heads + h] * LOG2E, F32)
        far_hi = far_c.astype(BF16).astype(F32)
        mask_hi = jnp.where(take, jnp.where(far, spread(far_hi), 0.0), NEG_INF)
        mask_lo = jnp.where(far, spread(far_c - far_hi), 0.0)
        qaug_scr[h] = jnp.concatenate(
            [q_t, mask_hi.astype(BF16), unused, mask_lo.astype(BF16), unused], axis=0)

    ones_rows = jnp.ones((BF16_ROWS, tile), BF16)

    def accumulate(step, slot, h):
        s = s_scr[slot, h].reshape(tile // sub, sub, tile)
        m_prev = m_scr[h]
        m_tile = jnp.max(tmax_scr[slot, h], axis=0, keepdims=True)
        m_next = jnp.maximum(m_prev, jnp.broadcast_to(m_tile, m_prev.shape))
        alpha = jnp.exp2(m_prev - m_next)
        p = jnp.exp2(s - m_next[None])
        m_scr[h] = m_next
        vt_aug = jnp.concatenate([vt_ref[ti - step, h * HEAD_DIM:(h + 1) * HEAD_DIM, :], ones_rows],
                                 axis=0)
        pv = jnp.dot(vt_aug, p.reshape(tile, tile).astype(BF16),
                     preferred_element_type=F32)
        acc = acc_scr[h].reshape(ACC_ROWS // sub, sub, tile) * alpha[None]
        acc_scr[h] = acc.reshape(ACC_ROWS, tile) + pv

    def loop_body(next_with_bias):
        def one_step(step, slot):
            for h in range(n_heads):
                scores(step + 1, 1 - slot, h, next_with_bias)
                accumulate(step, slot, h)

        def body(step, carry):
            for slot in (0, 1):
                pl.when((step & 1) == slot)(functools.partial(one_step, step, slot))
            return carry
        return body

    n_biased = jnp.minimum(ti, MOBA_NEAR_STEPS - 1)
    for h in range(n_heads):
        own_scores(h)
    m_scr[...] = jnp.full(m_scr.shape, NEG_INF, F32)

    @pl.when(ti == 0)
    def _():
        acc_scr[...] = jnp.zeros(acc_scr.shape, F32)
    lax.fori_loop(0, n_biased, loop_body(True), 0)
    lax.fori_loop(n_biased, ti, loop_body(False), 0)
    for h in range(n_heads):
        accumulate(ti, ti & 1, h)
        cols = slice(h * HEAD_DIM, (h + 1) * HEAD_DIM)
        z = z_ref[:, cols].astype(F32)
        inv = 1.0 / acc_scr[h, HEAD_DIM:HEAD_DIM + 1, :]
        out = (acc_scr[h, :HEAD_DIM, :] * inv).T
        o_ref[:, cols] = (z * jax.nn.sigmoid(z) * out).astype(BF16)


def _moba(proj, q_t, v_t, kmean, bias_tiles, rel_bias, batch, seq):
    m = proj.shape[0]
    tile = MOBA_TILE
    nq = seq // tile
    row_blk = np.arange(seq)[:, None] // MOBA_BLOCK
    onehot = jnp.asarray((np.arange(HEAD_DIM)[None, :] & (SEL_LANES - 1)) == row_blk, BF16)
    rows = lambda sec: pl.BlockSpec((tile, GROUP_WIDTH), lambda b, i: (b * nq + i, sec))
    whole = lambda sec: pl.BlockSpec((seq, GROUP_WIDTH), lambda b, i: (b, sec))
    return pl.pallas_call(
        _moba_kernel,
        grid=(batch, nq),
        in_specs=[pl.BlockSpec(memory_space=pltpu.SMEM),
                  pl.BlockSpec((None, GROUP_WIDTH, tile), lambda b, i: (b * nq + i, 0, 0)),
                  whole(0),
                  pl.BlockSpec((nq, GROUP_WIDTH, tile), lambda b, i: (b, 0, 0),
                               pipeline_mode=pl.Buffered(1)),
                  rows(1),
                  pl.BlockSpec((1, SEL_LANES, GROUP_WIDTH), lambda b, i: (b, 0, 0)),
                  pl.BlockSpec(onehot.shape, lambda b, i: (0, 0), pipeline_mode=pl.Buffered(1)),
                  pl.BlockSpec(bias_tiles.shape, lambda b, i: (0, 0, 0, 0),
                               pipeline_mode=pl.Buffered(1))],
        out_specs=pl.BlockSpec((tile, GROUP_WIDTH), lambda b, i: (b * nq + i, 0)),
        out_shape=jax.ShapeDtypeStruct((m, GROUP_WIDTH), BF16),
        scratch_shapes=[pltpu.VMEM((MOBA_HEADS, 2 * HEAD_DIM, tile), BF16),
                        pltpu.VMEM((MOBA_HEADS, SUBLANES, tile), F32),
                        pltpu.VMEM((MOBA_HEADS, ACC_ROWS, tile), F32),
                        pltpu.VMEM((2, MOBA_HEADS, tile, tile), F32),
                        pltpu.VMEM((2, MOBA_HEADS, SUBLANES, tile), F32)],
        compiler_params=pltpu.CompilerParams(
            dimension_semantics=("arbitrary", "arbitrary"), vmem_limit_bytes=VMEM_LIMIT),
        name="moba",
    )(rel_bias.astype(F32).reshape(-1), q_t, proj, v_t, proj, kmean, onehot, bias_tiles)


def _outproj_kernel(x_ref, r_ref, a_ref, wo_ref, p_ref, wp_ref, pg_ref, wg_ref, o_ref,
                    wo_scr, wp_scr, wg_scr):
    _cast_weights_once([(wo_ref, wo_scr), (wp_ref, wp_scr), (wg_ref, wg_scr)])

    def unit_rms(t):
        return t * lax.rsqrt(jnp.mean(t * t, axis=-1, keepdims=True) + EPS)

    slab = x_ref.shape[0] // OUT_SLABS
    slabs = [slice(c * slab, (c + 1) * slab) for c in range(OUT_SLABS)]
    for rows in slabs:
        y = (jnp.dot(r_ref[rows, :], wo_scr[:GROUP_WIDTH, :], preferred_element_type=F32)
             + jnp.dot(a_ref[rows, :], wo_scr[GROUP_WIDTH:, :], preferred_element_type=F32))
        o_ref[rows, :] = x_ref[rows, :] + y
    for rows in slabs:
        e = unit_rms(jnp.dot(p_ref[rows, :].astype(BF16), wp_scr[...], preferred_element_type=F32))
        x1 = o_ref[rows, :]
        gate = jax.nn.sigmoid(jnp.dot(unit_rms(x1).astype(BF16), wg_scr[...],
                                      preferred_element_type=F32))
        o_ref[rows, :] = x1 + gate * (e * pg_ref[...])


def _outproj(x2, r_out, m_out, layer, w_out, p3, w_ple, ple_g, w_gate):
    m, d_model = x2.shape
    tm = OUT_ROW_TILE
    rows = lambda a: pl.BlockSpec((tm, a.shape[1]), lambda i: (i, 0))
    full = lambda a: pl.BlockSpec((None,) + a.shape[1:], lambda i: (layer, 0, 0),
                                  pipeline_mode=pl.Buffered(1))
    return pl.pallas_call(
        _outproj_kernel,
        grid=(m // tm,),
        in_specs=[rows(x2), rows(r_out), rows(m_out), full(w_out),
                  pl.BlockSpec((None, tm, p3.shape[2]), lambda i: (layer, i, 0)), full(w_ple),
                  full(ple_g), full(w_gate)],
        out_specs=pl.BlockSpec((tm, d_model), lambda i: (i, 0)),
        out_shape=jax.ShapeDtypeStruct((m, d_model), F32),
        scratch_shapes=[pltpu.VMEM(w.shape[1:], BF16) for w in (w_out, w_ple, w_gate)],
        compiler_params=pltpu.CompilerParams(
            dimension_semantics=("arbitrary",), vmem_limit_bytes=VMEM_LIMIT),
        name="out_projection",
    )(x2, r_out, m_out, w_out, p3, w_ple, ple_g, w_gate)


def _rotary_tables(seq):
    half = HEAD_DIM // 2
    inv = np.float32(ROPE_THETA) ** (-np.arange(half, dtype=np.float32) / np.float32(half))
    ang = np.arange(seq, dtype=np.float32)[:, None] * inv[None, :]
    cos, sin = np.cos(ang), np.sin(ang)
    return (jnp.asarray(np.concatenate([cos, cos], axis=1), F32),
            jnp.asarray(np.concatenate([-sin, sin], axis=1), F32))


def kernel(x, p, norm_g, w_in, ret_norm_g, q_norm_g, k_norm_g, rel_bias, w_out, w_ple, ple_norm_g, w_ple_gate):
    batch, seq, d_model = x.shape
    depth = w_in.shape[0]
    m = batch * seq
    assert w_in.shape[2] == N_SECTIONS * GROUP_WIDTH and w_out.shape[1] == 2 * GROUP_WIDTH
    assert rel_bias.shape == (N_BUCKETS, MOBA_HEADS)
    assert seq % ROW_TILE == 0 and ROW_TILE % MOBA_BLOCK == 0 and seq % MOBA_TILE == 0
    assert m % OUT_ROW_TILE == 0
    assert ROW_TILE == MOBA_TILE
    assert ROW_TILE % RET_CHUNK == 0 and seq // MOBA_BLOCK <= SEL_LANES

    cos, sin = _rotary_tables(seq)
    bias_tiles = _bias_tiles(rel_bias)
    x2 = x.reshape(m, d_model)
    nb = seq // MOBA_BLOCK
    stack = lambda g: g.reshape(depth, 1, -1)
    p3 = p.reshape(depth, m, -1)
    for i in range(depth):
        proj, kmean, r_out, q_t, v_t = _inproj(x2, i, stack(norm_g), w_in, cos, sin,
                                     stack(q_norm_g), stack(k_norm_g), stack(ret_norm_g), seq)
        kmean = jnp.pad(kmean.reshape(batch, nb, GROUP_WIDTH), ((0, 0), (0, SEL_LANES - nb), (0, 0)))
        m_out = _moba(proj, q_t, v_t, kmean, bias_tiles, rel_bias, batch, seq)
        x2 = _outproj(x2, r_out, m_out, i, w_out, p3, w_ple, stack(ple_norm_g), w_ple_gate)
    return x2.reshape(batch, seq, d_model)
```

```python
import functools
import math

import numpy as np
import jax
import jax.numpy as jnp
from jax import lax
from jax.experimental import pallas as pl
from jax.experimental.pallas import tpu as pltpu

F32 = jnp.float32
BF16 = jnp.bfloat16

HEAD_DIM = 128
SUBLANES = 8
BF16_ROWS = 16
ACC_ROWS = HEAD_DIM + BF16_ROWS
RET_HEADS = 4
MOBA_HEADS = 4
GROUP_WIDTH = 512
N_SECTIONS = 8
RET_SECTIONS = 4
MOBA_T_SECTIONS = (4, 6)
MOBA_PROJ_SECTIONS = (5, 7)
RET_CHUNK = 128
MOBA_BLOCK = 256
MOBA_TOPK = 3
N_BUCKETS = 32
MAX_DISTANCE = 2048
ROPE_THETA = 10000.0
EPS = 1e-6
NEG_INF = -1e30
MASKED_GATE = -3e38
LOG2E = math.log2(math.e)
SEL_LANES = HEAD_DIM // 2

ROW_TILE = 512
OUT_ROW_TILE = 1024
OUT_SLABS = 4
CAST_ROWS = 256
VMEM_LIMIT = 56 * 1024 * 1024


def _bucket_thresholds():
    max_exact = N_BUCKETS // 2
    thr = []
    for k in range(N_BUCKETS - max_exact):
        n = max_exact
        while min(max_exact + int(math.log(n / max_exact) / math.log(MAX_DISTANCE / max_exact)
                                  * (N_BUCKETS - max_exact)), N_BUCKETS - 1) < max_exact + k:
            n += 1
        thr.append(n)
    return thr


_BUCKET_THR = _bucket_thresholds()
N_NEAR_BLOCKS = next(d for d in range(1, 1 << 20) if (d - 1) * MOBA_BLOCK + 1 >= _BUCKET_THR[-1])
MOBA_TILE = 2 * MOBA_BLOCK
MOBA_NEAR_STEPS = N_NEAR_BLOCKS // 2 + 1


def _bias_kernel(rb_ref, out_ref):
    h = pl.program_id(0)
    n_heads = pl.num_programs(0)
    c = lax.broadcasted_iota(jnp.int32, (MOBA_BLOCK, MOBA_BLOCK), 0)
    r = lax.broadcasted_iota(jnp.int32, (MOBA_BLOCK, MOBA_BLOCK), 1)
    max_exact = N_BUCKETS // 2

    def rb(b):
        return rb_ref[b * n_heads + h] * LOG2E

    for d in range(N_NEAR_BLOCKS):
        n = d * MOBA_BLOCK + r - c
        nn = jnp.maximum(n, 0)
        val = jnp.full((MOBA_BLOCK, MOBA_BLOCK), rb(0), F32)
        for b in range(1, max_exact):
            val = jnp.where(nn >= b, rb(b), val)
        for k, thr in enumerate(_BUCKET_THR):
            val = jnp.where(nn >= thr, rb(max_exact + k), val)
        if d == 0:
            val = jnp.where(n >= 0, val, NEG_INF)
        out_ref[0, d] = val
    out_ref[0, N_NEAR_BLOCKS] = jnp.zeros((MOBA_BLOCK, MOBA_BLOCK), F32)


def _bias_tiles(rel_bias):
    n_heads = rel_bias.shape[1]
    return pl.pallas_call(
        _bias_kernel,
        grid=(n_heads,),
        in_specs=[pl.BlockSpec(memory_space=pltpu.SMEM)],
        out_specs=pl.BlockSpec((1, N_NEAR_BLOCKS + 1, MOBA_BLOCK, MOBA_BLOCK), lambda h: (h, 0, 0, 0)),
        out_shape=jax.ShapeDtypeStruct((n_heads, N_NEAR_BLOCKS + 1, MOBA_BLOCK, MOBA_BLOCK), F32),
        name="bias_tiles",
    )(rel_bias.astype(F32).reshape(-1))


def _cast_weights_once(pairs):
    @pl.when(pl.program_id(0) == 0)
    def _():
        for w_ref, w_scr in pairs:
            for r in range(0, w_ref.shape[0], CAST_ROWS):
                w_scr[r:r + CAST_ROWS, :] = w_ref[r:r + CAST_ROWS, :].astype(BF16)


def _inproj_kernel(x_ref, g_ref, w_ref, cos_ref, sin_ref, qg_ref, kg_ref,
                   dec_ref, dq_ref, dk_ref, cd_ref, rg_ref,
                   proj_ref, kmean_ref, ret_ref, qt_ref, vt_ref, w_scr, rin_scr, kt_scr, state_ref,
                   *, seq_tiles):
    _cast_weights_once([(w_ref, w_scr)])
    x = x_ref[...]
    ms = jnp.mean(x * x, axis=-1, keepdims=True)
    h = (x * lax.rsqrt(ms + EPS) * g_ref[...]).astype(BF16)
    cos = cos_ref[...]
    sin = sin_ref[...]
    scale = HEAD_DIM ** -0.5
    rows = x.shape[0]
    heads = GROUP_WIDTH // HEAD_DIM

    def head_norm(y, gain):
        return y * lax.rsqrt(jnp.mean(y * y, axis=-1, keepdims=True) + EPS) * gain

    def section(s):
        return jnp.dot(h, w_scr[:, s * GROUP_WIDTH:(s + 1) * GROUP_WIDTH], preferred_element_type=F32)

    for s in range(RET_SECTIONS):
        y = section(s)
        for hh in range(heads):
            yh = y[:, hh * HEAD_DIM:(hh + 1) * HEAD_DIM]
            if s in (0, 1):
                yh = yh * cos + pltpu.roll(yh, HEAD_DIM // 2, axis=1) * sin
                if s == 1:
                    yh = yh * scale
            if s == 1:
                kt_scr[hh] = yh.T.astype(BF16)
            else:
                col = s * GROUP_WIDTH + hh * HEAD_DIM
                rin_scr[:, col:col + HEAD_DIM] = yh.astype(BF16)

    @pl.when(pl.program_id(0) % seq_tiles == 0)
    def _():
        state_ref[...] = jnp.zeros_like(state_ref)

    def moba_section(s):
        y = section(s)
        for hh in range(heads):
            yh = y[:, hh * HEAD_DIM:(hh + 1) * HEAD_DIM]
            if s == 4:
                yh = head_norm(yh, qg_ref[...]) * (scale * LOG2E)
            elif s == 5:
                yh = head_norm(yh, kg_ref[...])
                for blk in range(rows // MOBA_BLOCK):
                    kmean_ref[blk, :, hh * HEAD_DIM:(hh + 1) * HEAD_DIM] = jnp.mean(
                        yh[blk * MOBA_BLOCK:(blk + 1) * MOBA_BLOCK], axis=0, keepdims=True)
            if s in MOBA_T_SECTIONS:
                t_ref = qt_ref if s == MOBA_T_SECTIONS[0] else vt_ref
                t_ref[hh * HEAD_DIM:(hh + 1) * HEAD_DIM, :] = yh.T.astype(BF16)
            else:
                col = MOBA_PROJ_SECTIONS.index(s) * GROUP_WIDTH + hh * HEAD_DIM
                proj_ref[:, col:col + HEAD_DIM] = yh.astype(BF16)

    chunk = RET_CHUNK
    units = [(c, hh) for c in range(rows // chunk) for hh in range(RET_HEADS)]

    def operand(c, hh, sec):
        return rin_scr[c * chunk:(c + 1) * chunk,
                       sec * GROUP_WIDTH + hh * HEAD_DIM:sec * GROUP_WIDTH + (hh + 1) * HEAD_DIM]

    masked, kv = {}, {}
    for c, hh in units:
        q, v = operand(c, hh, 0), operand(c, hh, 2)
        k_t = kt_scr[hh, :, c * chunk:(c + 1) * chunk]
        sc = jnp.dot(q, k_t, preferred_element_type=F32) * dec_ref[hh]
        masked[c, hh] = sc.astype(BF16)
        v_dec = (v.astype(F32) * dk_ref[hh]).astype(BF16)
        kv[c, hh] = jnp.dot(k_t, v_dec, preferred_element_type=F32)
    before = {}
    for hh in range(RET_HEADS):
        state = state_ref[hh]
        for c in range(rows // chunk):
            before[c, hh] = state.astype(BF16)
            state = state * cd_ref[hh] + kv[c, hh]
        state_ref[hh] = state
    for c, hh in units:
        q, v = operand(c, hh, 0), operand(c, hh, 2)
        q_dec = (q.astype(F32) * dq_ref[hh]).astype(BF16)
        o = jnp.dot(jnp.concatenate([masked[c, hh], q_dec], axis=1),
                    jnp.concatenate([v, before[c, hh]], axis=0), preferred_element_type=F32)
        mu = jnp.mean(o, axis=-1, keepdims=True)
        oc = o - mu
        var = jnp.mean(oc * oc, axis=-1, keepdims=True)
        on = oc * lax.rsqrt(var + EPS) * rg_ref[:, hh * HEAD_DIM:(hh + 1) * HEAD_DIM]
        z = operand(c, hh, 3).astype(F32)
        ret_ref[c * chunk:(c + 1) * chunk, hh * HEAD_DIM:(hh + 1) * HEAD_DIM] = (
            z * jax.nn.sigmoid(z) * on).astype(BF16)

    for s in range(RET_SECTIONS, N_SECTIONS):
        moba_section(s)


def _retention_tables():
    c = RET_CHUNK
    log_g = np.log1p(-np.power(2.0, -5.0 - np.arange(RET_HEADS, dtype=np.float64)))
    j = np.arange(c, dtype=np.float64)
    rel = j[:, None] - j[None, :]
    decay = np.where(rel >= 0, np.exp(log_g[:, None, None] * np.maximum(rel, 0.0)[None]), 0.0)
    dq = np.exp(log_g[:, None] * (j + 1.0)[None])
    dk = np.exp(log_g[:, None] * (c - 1 - j)[None])
    cd = np.exp(log_g * c)
    wide = lambda t: np.broadcast_to(t[:, :, None], (RET_HEADS, c, HEAD_DIM))
    return (jnp.asarray(decay, F32), jnp.asarray(wide(dq), F32), jnp.asarray(wide(dk), F32),
            jnp.asarray(np.broadcast_to(cd[:, None, None], (RET_HEADS, 1, HEAD_DIM)), F32))


def _inproj(x2, layer, norm_g, w_in, cos, sin, q_g, k_g, ret_g, seq):
    m, d_model = x2.shape
    width = w_in.shape[2]
    moba_width = len(MOBA_PROJ_SECTIONS) * GROUP_WIDTH
    tm = ROW_TILE
    seq_tiles = seq // tm
    tables = _retention_tables()
    full = lambda a: pl.BlockSpec(a.shape, lambda i: (0,) * a.ndim)
    row1 = lambda n: pl.BlockSpec((None, 1, n), lambda i: (layer, 0, 0))
    return pl.pallas_call(
        functools.partial(_inproj_kernel, seq_tiles=seq_tiles),
        grid=(m // tm,),
        in_specs=[
            pl.BlockSpec((tm, d_model), lambda i: (i, 0)),
            row1(d_model),
            pl.BlockSpec((None, d_model, width), lambda i: (layer, 0, 0),
                         pipeline_mode=pl.Buffered(1)),
            pl.BlockSpec((tm, HEAD_DIM), lambda i: (i % seq_tiles, 0)),
            pl.BlockSpec((tm, HEAD_DIM), lambda i: (i % seq_tiles, 0)),
            row1(HEAD_DIM), row1(HEAD_DIM),
            *[full(t) for t in tables],
            row1(GROUP_WIDTH),
        ],
        out_specs=[
            pl.BlockSpec((tm, moba_width), lambda i: (i, 0)),
            pl.BlockSpec((tm // MOBA_BLOCK, 1, GROUP_WIDTH), lambda i: (i, 0, 0)),
            pl.BlockSpec((tm, GROUP_WIDTH), lambda i: (i, 0)),
            pl.BlockSpec((None, GROUP_WIDTH, tm), lambda i: (i, 0, 0)),
            pl.BlockSpec((None, GROUP_WIDTH, tm), lambda i: (i, 0, 0)),
        ],
        out_shape=[
            jax.ShapeDtypeStruct((m, moba_width), BF16),
            jax.ShapeDtypeStruct((m // MOBA_BLOCK, 1, GROUP_WIDTH), F32),
            jax.ShapeDtypeStruct((m, GROUP_WIDTH), BF16),
            jax.ShapeDtypeStruct((m // tm, GROUP_WIDTH, tm), BF16),
            jax.ShapeDtypeStruct((m // tm, GROUP_WIDTH, tm), BF16),
        ],
        scratch_shapes=[pltpu.VMEM((d_model, width), BF16),
                        pltpu.VMEM((tm, RET_SECTIONS * GROUP_WIDTH), BF16),
                        pltpu.VMEM((RET_HEADS, HEAD_DIM, tm), BF16),
                        pltpu.VMEM((RET_HEADS, HEAD_DIM, HEAD_DIM), F32)],
        compiler_params=pltpu.CompilerParams(
            dimension_semantics=("arbitrary",), vmem_limit_bytes=VMEM_LIMIT),
        name="in_projection",
    )(x2, norm_g, w_in, cos, sin, q_g, k_g, *tables, ret_g)


def _moba_kernel(rb_ref, qt_ref, k_ref, vt_ref, z_ref, km_ref, oh_ref, bias_ref, o_ref,
                 qaug_scr, m_scr, acc_scr, s_scr, tmax_scr):
    blk = MOBA_BLOCK
    tile = MOBA_TILE
    sub = SUBLANES
    ti = pl.program_id(1)
    n_heads = MOBA_HEADS
    sel_rows = -(-(k_ref.shape[0] // blk) // BF16_ROWS) * BF16_ROWS
    blk_id = lax.broadcasted_iota(jnp.int32, (sel_rows, tile), 0)
    blk_f = blk_id.astype(F32)
    q_blk = 2 * ti + lax.broadcasted_iota(jnp.int32, (sel_rows, tile), 1) // blk
    unused = jnp.zeros((SEL_LANES - sel_rows, tile), BF16)
    past = blk_id < q_blk
    far = (q_blk - blk_id) >= N_NEAR_BLOCKS
    own = blk_id == q_blk
    spread = lambda a: jnp.broadcast_to(a[None], (sel_rows // sub,) + a.shape).reshape(sel_rows, tile)

    def key_rows(step):
        return pl.ds(pl.multiple_of((ti - step) * tile, tile), tile)

    def bias_tile(h, step):
        def part(offset):
            idx = jnp.where(offset < 0, N_NEAR_BLOCKS, jnp.minimum(offset, N_NEAR_BLOCKS))
            return bias_ref[h, idx]
        diag = part(2 * step)
        return jnp.concatenate([jnp.concatenate([diag, part(2 * step + 1)], axis=1),
                                jnp.concatenate([part(2 * step - 1), diag], axis=1)], axis=0)

    def key_tile(step, h):
        rows = key_rows(step)
        return jnp.concatenate([k_ref[rows, h * HEAD_DIM:(h + 1) * HEAD_DIM], oh_ref[rows, :]], axis=1)

    def scores(step, slot, h, with_bias):
        s = jnp.dot(key_tile(step, h), qaug_scr[h], preferred_element_type=F32)
        if with_bias:
            s = s + bias_tile(h, step)
        s_scr[slot, h] = s
        tmax_scr[slot, h] = jnp.max(s.reshape(tile // sub, sub, tile), axis=0)

    def own_scores(h):
        k_aug = key_tile(0, h)
        diag = bias_ref[h, 0]
        first = jnp.dot(k_aug[:blk], qaug_scr[h], preferred_element_type=F32)
        first = first + jnp.concatenate([diag, bias_ref[h, 1]], axis=1)
        second = jnp.dot(k_aug[blk:], qaug_scr[h, :, blk:], preferred_element_type=F32) + diag
        s_scr[0, h, :blk, :] = first
        s_scr[0, h, blk:, blk:] = second
        t_first = jnp.max(first.reshape(blk // sub, sub, tile), axis=0)
        t_second = jnp.max(second.reshape(blk // sub, sub, blk), axis=0)
        tmax_scr[0, h] = jnp.concatenate(
            [t_first[:, :blk], jnp.maximum(t_first[:, blk:], t_second)], axis=1)

    s_scr[0, :, blk:, :blk] = jnp.full((n_heads, blk, blk), NEG_INF, F32)

    for h in range(n_heads):
        cols = slice(h * HEAD_DIM, (h + 1) * HEAD_DIM)
        q_t = qt_ref[cols, :]
        km = km_ref[0, :sel_rows, cols]
        km_hi = km.astype(BF16)
        km_lo = (km - km_hi.astype(F32)).astype(BF16)
        gate2 = jnp.dot(jnp.concatenate([km_hi, km_lo], axis=0), q_t, preferred_element_type=F32)
        gate = gate2[:sel_rows] + gate2[sel_rows:]
        g = jnp.where(past, gate, MASKED_GATE)
        for _ in range(MOBA_TOPK):
            top = jnp.max(g, axis=0, keepdims=True)
            first = jnp.min(jnp.where(g == top, blk_f, 1e9), axis=0, keepdims=True)
            g = jnp.where(blk_f == first, MASKED_GATE, g)
        take = (past & (g == MASKED_GATE)) | own
        far_c = jnp.full((sub, tile), rb_ref[(N_BUCKETS - 1) * n_heads + h] * LOG2E, F32)
        far_hi = far_c.astype(BF16).astype(F32)
        mask_hi = jnp.where(take, jnp.where(far, spread(far_hi), 0.0), NEG_INF)
        mask_lo = jnp.where(far, spread(far_c - far_hi), 0.0)
        qaug_scr[h] = jnp.concatenate(
            [q_t, mask_hi.astype(BF16), unused, mask_lo.astype(BF16), unused], axis=0)

    ones_rows = jnp.ones((BF16_ROWS, tile), BF16)

    def accumulate(step, slot, h):
        s = s_scr[slot, h].reshape(tile // sub, sub, tile)
        m_prev = m_scr[h]
        m_tile = jnp.max(tmax_scr[slot, h], axis=0, keepdims=True)
        m_next = jnp.maximum(m_prev, jnp.broadcast_to(m_tile, m_prev.shape))
        alpha = jnp.exp2(m_prev - m_next)
        p = jnp.exp2(s - m_next[None])
        m_scr[h] = m_next
        vt_aug = jnp.concatenate([vt_ref[ti - step, h * HEAD_DIM:(h + 1) * HEAD_DIM, :], ones_rows],
                                 axis=0)
        pv = jnp.dot(vt_aug, p.reshape(tile, tile).astype(BF16),
                     preferred_element_type=F32)
        acc = acc_scr[h].reshape(ACC_ROWS // sub, sub, tile) * alpha[None]
        acc_scr[h] = acc.reshape(ACC_ROWS, tile) + pv

    def loop_body(next_with_bias):
        def one_step(step, slot):
            for h in range(n_heads):
                scores(step + 1, 1 - slot, h, next_with_bias)
                accumulate(step, slot, h)

        def body(step, carry):
            for slot in (0, 1):
                pl.when((step & 1) == slot)(functools.partial(one_step, step, slot))
            return carry
        return body

    n_biased = jnp.minimum(ti, MOBA_NEAR_STEPS - 1)
    for h in range(n_heads):
        own_scores(h)
    m_scr[...] = jnp.full(m_scr.shape, NEG_INF, F32)

    @pl.when(ti == 0)
    def _():
        acc_scr[...] = jnp.zeros(acc_scr.shape, F32)
    lax.fori_loop(0, n_biased, loop_body(True), 0)
    lax.fori_loop(n_biased, ti, loop_body(False), 0)
    for h in range(n_heads):
        accumulate(ti, ti & 1, h)
        cols = slice(h * HEAD_DIM, (h + 1) * HEAD_DIM)
        z = z_ref[:, cols].astype(F32)
        inv = 1.0 / acc_scr[h, HEAD_DIM:HEAD_DIM + 1, :]
        out = (acc_scr[h, :HEAD_DIM, :] * inv).T
        half_z = 0.5 * z
        o_ref[:, cols] = ((half_z + half_z * jnp.tanh(half_z)) * out).astype(BF16)


def _moba(proj, q_t, v_t, kmean, bias_tiles, rel_bias, batch, seq):
    m = proj.shape[0]
    tile = MOBA_TILE
    nq = seq // tile
    row_blk = np.arange(seq)[:, None] // MOBA_BLOCK
    onehot = jnp.asarray((np.arange(HEAD_DIM)[None, :] & (SEL_LANES - 1)) == row_blk, BF16)
    rows = lambda sec: pl.BlockSpec((tile, GROUP_WIDTH), lambda b, i: (b * nq + i, sec))
    whole = lambda sec: pl.BlockSpec((seq, GROUP_WIDTH), lambda b, i: (b, sec))
    return pl.pallas_call(
        _moba_kernel,
        grid=(batch, nq),
        in_specs=[pl.BlockSpec(memory_space=pltpu.SMEM),
                  pl.BlockSpec((None, GROUP_WIDTH, tile), lambda b, i: (b * nq + i, 0, 0)),
                  whole(0),
                  pl.BlockSpec((nq, GROUP_WIDTH, tile), lambda b, i: (b, 0, 0),
                               pipeline_mode=pl.Buffered(1)),
                  rows(1),
                  pl.BlockSpec((1, SEL_LANES, GROUP_WIDTH), lambda b, i: (b, 0, 0)),
                  pl.BlockSpec(onehot.shape, lambda b, i: (0, 0), pipeline_mode=pl.Buffered(1)),
                  pl.BlockSpec(bias_tiles.shape, lambda b, i: (0, 0, 0, 0),
                               pipeline_mode=pl.Buffered(1))],
        out_specs=pl.BlockSpec((tile, GROUP_WIDTH), lambda b, i: (b * nq + i, 0)),
        out_shape=jax.ShapeDtypeStruct((m, GROUP_WIDTH), BF16),
        scratch_shapes=[pltpu.VMEM((MOBA_HEADS, 2 * HEAD_DIM, tile), BF16),
                        pltpu.VMEM((MOBA_HEADS, SUBLANES, tile), F32),
                        pltpu.VMEM((MOBA_HEADS, ACC_ROWS, tile), F32),
                        pltpu.VMEM((2, MOBA_HEADS, tile, tile), F32),
                        pltpu.VMEM((2, MOBA_HEADS, SUBLANES, tile), F32)],
        compiler_params=pltpu.CompilerParams(
            dimension_semantics=("arbitrary", "arbitrary"), vmem_limit_bytes=VMEM_LIMIT),
        name="moba",
    )(rel_bias.astype(F32).reshape(-1), q_t, proj, v_t, proj, kmean, onehot, bias_tiles)


def _outproj_kernel(x_ref, r_ref, a_ref, wo_ref, p_ref, wp_ref, pg_ref, wg_ref, o_ref,
                    wo_scr, wp_scr, wg_scr):
    _cast_weights_once([(wo_ref, wo_scr), (wp_ref, wp_scr), (wg_ref, wg_scr)])

    def unit_rms(t):
        return t * lax.rsqrt(jnp.mean(t * t, axis=-1, keepdims=True) + EPS)

    slab = x_ref.shape[0] // OUT_SLABS
    slabs = [slice(c * slab, (c + 1) * slab) for c in range(OUT_SLABS)]
    for rows in slabs:
        y = (jnp.dot(r_ref[rows, :], wo_scr[:GROUP_WIDTH, :], preferred_element_type=F32)
             + jnp.dot(a_ref[rows, :], wo_scr[GROUP_WIDTH:, :], preferred_element_type=F32))
        o_ref[rows, :] = x_ref[rows, :] + y
    for rows in slabs:
        e = unit_rms(jnp.dot(p_ref[rows, :].astype(BF16), wp_scr[...], preferred_element_type=F32))
        x1 = o_ref[rows, :]
        gate = jax.nn.sigmoid(jnp.dot(unit_rms(x1).astype(BF16), wg_scr[...],
                                      preferred_element_type=F32))
        o_ref[rows, :] = x1 + gate * (e * pg_ref[...])


def _outproj(x2, r_out, m_out, layer, w_out, p3, w_ple, ple_g, w_gate):
    m, d_model = x2.shape
    tm = OUT_ROW_TILE
    rows = lambda a: pl.BlockSpec((tm, a.shape[1]), lambda i: (i, 0))
    full = lambda a: pl.BlockSpec((None,) + a.shape[1:], lambda i: (layer, 0, 0),
                                  pipeline_mode=pl.Buffered(1))
    return pl.pallas_call(
        _outproj_kernel,
        grid=(m // tm,),
        in_specs=[rows(x2), rows(r_out), rows(m_out), full(w_out),
                  pl.BlockSpec((None, tm, p3.shape[2]), lambda i: (layer, i, 0)), full(w_ple),
                  full(ple_g), full(w_gate)],
        out_specs=pl.BlockSpec((tm, d_model), lambda i: (i, 0)),
        out_shape=jax.ShapeDtypeStruct((m, d_model), F32),
        scratch_shapes=[pltpu.VMEM(w.shape[1:], BF16) for w in (w_out, w_ple, w_gate)],
        compiler_params=pltpu.CompilerParams(
            dimension_semantics=("arbitrary",), vmem_limit_bytes=VMEM_LIMIT),
        name="out_projection",
    )(x2, r_out, m_out, w_out, p3, w_ple, ple_g, w_gate)


def _rotary_tables(seq):
    half = HEAD_DIM // 2
    inv = np.float32(ROPE_THETA) ** (-np.arange(half, dtype=np.float32) / np.float32(half))
    ang = np.arange(seq, dtype=np.float32)[:, None] * inv[None, :]
    cos, sin = np.cos(ang), np.sin(ang)
    return (jnp.asarray(np.concatenate([cos, cos], axis=1), F32),
            jnp.asarray(np.concatenate([-sin, sin], axis=1), F32))


def kernel(x, p, norm_g, w_in, ret_norm_g, q_norm_g, k_norm_g, rel_bias, w_out, w_ple, ple_norm_g, w_ple_gate):
    batch, seq, d_model = x.shape
    depth = w_in.shape[0]
    m = batch * seq
    assert w_in.shape[2] == N_SECTIONS * GROUP_WIDTH and w_out.shape[1] == 2 * GROUP_WIDTH
    assert rel_bias.shape == (N_BUCKETS, MOBA_HEADS)
    assert seq % ROW_TILE == 0 and ROW_TILE % MOBA_BLOCK == 0 and seq % MOBA_TILE == 0
    assert m % OUT_ROW_TILE == 0
    assert ROW_TILE == MOBA_TILE
    assert ROW_TILE % RET_CHUNK == 0 and seq // MOBA_BLOCK <= SEL_LANES

    cos, sin = _rotary_tables(seq)
    bias_tiles = _bias_tiles(rel_bias)
    x2 = x.reshape(m, d_model)
    nb = seq // MOBA_BLOCK
    stack = lambda g: g.reshape(depth, 1, -1)
    p3 = p.reshape(depth, m, -1)
    for i in range(depth):
        proj, kmean, r_out, q_t, v_t = _inproj(x2, i, stack(norm_g), w_in, cos, sin,
                                     stack(q_norm_g), stack(k_norm_g), stack(ret_norm_g), seq)
        kmean = jnp.pad(kmean.reshape(batch, nb, GROUP_WIDTH), ((0, 0), (0, SEL_LANES - nb), (0, 0)))
        m_out = _moba(proj, q_t, v_t, kmean, bias_tiles, rel_bias, batch, seq)
        x2 = _outproj(x2, r_out, m_out, i, w_out, p3, w_ple, stack(ple_norm_g), w_ple_gate)
    return x2.reshape(batch, seq, d_model)
```

```python
import functools
import math

import numpy as np
import jax
import jax.numpy as jnp
from jax import lax
from jax.experimental import pallas as pl
from jax.experimental.pallas import tpu as pltpu

F32 = jnp.float32
BF16 = jnp.bfloat16

HEAD_DIM = 128
SUBLANES = 8
BF16_ROWS = 16
ACC_ROWS = HEAD_DIM + BF16_ROWS
RET_HEADS = 4
MOBA_HEADS = 4
GROUP_WIDTH = 512
N_SECTIONS = 8
RET_SECTIONS = 4
MOBA_T_SECTIONS = (4, 6)
MOBA_PROJ_SECTIONS = (5, 7)
RET_CHUNK = 128
MOBA_BLOCK = 256
MOBA_TOPK = 3
N_BUCKETS = 32
MAX_DISTANCE = 2048
ROPE_THETA = 10000.0
EPS = 1e-6
NEG_INF = -1e30
MASKED_GATE = -3e38
LOG2E = math.log2(math.e)
SEL_LANES = HEAD_DIM // 2

ROW_TILE = 512
OUT_ROW_TILE = 1024
OUT_SLABS = 4
CAST_ROWS = 256
VMEM_LIMIT = 56 * 1024 * 1024


def _bucket_thresholds():
    max_exact = N_BUCKETS // 2
    thr = []
    for k in range(N_BUCKETS - max_exact):
        n = max_exact
        while min(max_exact + int(math.log(n / max_exact) / math.log(MAX_DISTANCE / max_exact)
                                  * (N_BUCKETS - max_exact)), N_BUCKETS - 1) < max_exact + k:
            n += 1
        thr.append(n)
    return thr


_BUCKET_THR = _bucket_thresholds()
N_NEAR_BLOCKS = next(d for d in range(1, 1 << 20) if (d - 1) * MOBA_BLOCK + 1 >= _BUCKET_THR[-1])
MOBA_TILE = 2 * MOBA_BLOCK
MOBA_NEAR_STEPS = N_NEAR_BLOCKS // 2 + 1


def _bias_kernel(rb_ref, out_ref):
    h = pl.program_id(0)
    n_heads = pl.num_programs(0)
    c = lax.broadcasted_iota(jnp.int32, (MOBA_BLOCK, MOBA_BLOCK), 0)
    r = lax.broadcasted_iota(jnp.int32, (MOBA_BLOCK, MOBA_BLOCK), 1)
    max_exact = N_BUCKETS // 2

    def rb(b):
        return rb_ref[b * n_heads + h] * LOG2E

    for d in range(N_NEAR_BLOCKS):
        n = d * MOBA_BLOCK + r - c
        nn = jnp.maximum(n, 0)
        val = jnp.full((MOBA_BLOCK, MOBA_BLOCK), rb(0), F32)
        for b in range(1, max_exact):
            val = jnp.where(nn >= b, rb(b), val)
        for k, thr in enumerate(_BUCKET_THR):
            val = jnp.where(nn >= thr, rb(max_exact + k), val)
        if d == 0:
            val = jnp.where(n >= 0, val, NEG_INF)
        out_ref[0, d] = val
    out_ref[0, N_NEAR_BLOCKS] = jnp.zeros((MOBA_BLOCK, MOBA_BLOCK), F32)


def _bias_tiles(rel_bias):
    n_heads = rel_bias.shape[1]
    return pl.pallas_call(
        _bias_kernel,
        grid=(n_heads,),
        in_specs=[pl.BlockSpec(memory_space=pltpu.SMEM)],
        out_specs=pl.BlockSpec((1, N_NEAR_BLOCKS + 1, MOBA_BLOCK, MOBA_BLOCK), lambda h: (h, 0, 0, 0)),
        out_shape=jax.ShapeDtypeStruct((n_heads, N_NEAR_BLOCKS + 1, MOBA_BLOCK, MOBA_BLOCK), F32),
        name="bias_tiles",
    )(rel_bias.astype(F32).reshape(-1))


def _cast_weights_once(pairs):
    @pl.when(pl.program_id(0) == 0)
    def _():
        for w_ref, w_scr in pairs:
            for r in range(0, w_ref.shape[0], CAST_ROWS):
                w_scr[r:r + CAST_ROWS, :] = w_ref[r:r + CAST_ROWS, :].astype(BF16)


def _inproj_kernel(x_ref, g_ref, w_ref, cos_ref, sin_ref, qg_ref, kg_ref,
                   dec_ref, dq_ref, dk_ref, cd_ref, rg_ref,
                   proj_ref, kmean_ref, ret_ref, qt_ref, vt_ref, w_scr, rin_scr, kt_scr, state_ref,
                   *, seq_tiles):
    _cast_weights_once([(w_ref, w_scr)])
    x = x_ref[...]
    ms = jnp.mean(x * x, axis=-1, keepdims=True)
    h = (x * lax.rsqrt(ms + EPS) * g_ref[...]).astype(BF16)
    cos = cos_ref[...]
    sin = sin_ref[...]
    scale = HEAD_DIM ** -0.5
    rows = x.shape[0]
    heads = GROUP_WIDTH // HEAD_DIM

    def head_norm(y, gain):
        return y * lax.rsqrt(jnp.mean(y * y, axis=-1, keepdims=True) + EPS) * gain

    def section(s):
        return jnp.dot(h, w_scr[:, s * GROUP_WIDTH:(s + 1) * GROUP_WIDTH], preferred_element_type=F32)

    for s in range(RET_SECTIONS):
        y = section(s)
        for hh in range(heads):
            yh = y[:, hh * HEAD_DIM:(hh + 1) * HEAD_DIM]
            if s in (0, 1):
                yh = yh * cos + pltpu.roll(yh, HEAD_DIM // 2, axis=1) * sin
                if s == 1:
                    yh = yh * scale
            if s == 1:
                kt_scr[hh] = yh.T.astype(BF16)
            else:
                col = s * GROUP_WIDTH + hh * HEAD_DIM
                rin_scr[:, col:col + HEAD_DIM] = yh.astype(BF16)

    @pl.when(pl.program_id(0) % seq_tiles == 0)
    def _():
        state_ref[...] = jnp.zeros_like(state_ref)

    def moba_section(s):
        y = section(s)
        for hh in range(heads):
            yh = y[:, hh * HEAD_DIM:(hh + 1) * HEAD_DIM]
            if s == 4:
                yh = head_norm(yh, qg_ref[...]) * (scale * LOG2E)
            elif s == 5:
                yh = head_norm(yh, kg_ref[...])
                for blk in range(rows // MOBA_BLOCK):
                    kmean_ref[blk, :, hh * HEAD_DIM:(hh + 1) * HEAD_DIM] = jnp.mean(
                        yh[blk * MOBA_BLOCK:(blk + 1) * MOBA_BLOCK], axis=0, keepdims=True)
            if s in MOBA_T_SECTIONS:
                t_ref = qt_ref if s == MOBA_T_SECTIONS[0] else vt_ref
                t_ref[hh * HEAD_DIM:(hh + 1) * HEAD_DIM, :] = yh.T.astype(BF16)
            else:
                col = MOBA_PROJ_SECTIONS.index(s) * GROUP_WIDTH + hh * HEAD_DIM
                proj_ref[:, col:col + HEAD_DIM] = yh.astype(BF16)

    chunk = RET_CHUNK
    units = [(c, hh) for c in range(rows // chunk) for hh in range(RET_HEADS)]

    def operand(c, hh, sec):
        return rin_scr[c * chunk:(c + 1) * chunk,
                       sec * GROUP_WIDTH + hh * HEAD_DIM:sec * GROUP_WIDTH + (hh + 1) * HEAD_DIM]

    masked, kv = {}, {}
    for c, hh in units:
        q, v = operand(c, hh, 0), operand(c, hh, 2)
        k_t = kt_scr[hh, :, c * chunk:(c + 1) * chunk]
        sc = jnp.dot(q, k_t, preferred_element_type=F32) * dec_ref[hh]
        masked[c, hh] = sc.astype(BF16)
        v_dec = (v.astype(F32) * dk_ref[hh]).astype(BF16)
        kv[c, hh] = jnp.dot(k_t, v_dec, preferred_element_type=F32)
    before = {}
    for hh in range(RET_HEADS):
        state = state_ref[hh]
        for c in range(rows // chunk):
            before[c, hh] = state.astype(BF16)
            state = state * cd_ref[hh] + kv[c, hh]
        state_ref[hh] = state
    for c, hh in units:
        q, v = operand(c, hh, 0), operand(c, hh, 2)
        q_dec = (q.astype(F32) * dq_ref[hh]).astype(BF16)
        o = jnp.dot(jnp.concatenate([masked[c, hh], q_dec], axis=1),
                    jnp.concatenate([v, before[c, hh]], axis=0), preferred_element_type=F32)
        mu = jnp.mean(o, axis=-1, keepdims=True)
        oc = o - mu
        var = jnp.mean(oc * oc, axis=-1, keepdims=True)
        on = oc * lax.rsqrt(var + EPS) * rg_ref[:, hh * HEAD_DIM:(hh + 1) * HEAD_DIM]
        z = operand(c, hh, 3).astype(F32)
        ret_ref[c * chunk:(c + 1) * chunk, hh * HEAD_DIM:(hh + 1) * HEAD_DIM] = (
            z * jax.nn.sigmoid(z) * on).astype(BF16)

    for s in range(RET_SECTIONS, N_SECTIONS):
        moba_section(s)


def _retention_tables():
    c = RET_CHUNK
    log_g = np.log1p(-np.power(2.0, -5.0 - np.arange(RET_HEADS, dtype=np.float64)))
    j = np.arange(c, dtype=np.float64)
    rel = j[:, None] - j[None, :]
    decay = np.where(rel >= 0, np.exp(log_g[:, None, None] * np.maximum(rel, 0.0)[None]), 0.0)
    dq = np.exp(log_g[:, None] * (j + 1.0)[None])
    dk = np.exp(log_g[:, None] * (c - 1 - j)[None])
    cd = np.exp(log_g * c)
    wide = lambda t: np.broadcast_to(t[:, :, None], (RET_HEADS, c, HEAD_DIM))
    return (jnp.asarray(decay, F32), jnp.asarray(wide(dq), F32), jnp.asarray(wide(dk), F32),
            jnp.asarray(np.broadcast_to(cd[:, None, None], (RET_HEADS, 1, HEAD_DIM)), F32))


def _inproj(x2, layer, norm_g, w_in, cos, sin, q_g, k_g, ret_g, seq):
    m, d_model = x2.shape
    width = w_in.shape[2]
    moba_width = len(MOBA_PROJ_SECTIONS) * GROUP_WIDTH
    tm = ROW_TILE
    seq_tiles = seq // tm
    tables = _retention_tables()
    full = lambda a: pl.BlockSpec(a.shape, lambda i: (0,) * a.ndim)
    row1 = lambda n: pl.BlockSpec((None, 1, n), lambda i: (layer, 0, 0))
    return pl.pallas_call(
        functools.partial(_inproj_kernel, seq_tiles=seq_tiles),
        grid=(m // tm,),
        in_specs=[
            pl.BlockSpec((tm, d_model), lambda i: (i, 0)),
            row1(d_model),
            pl.BlockSpec((None, d_model, width), lambda i: (layer, 0, 0),
                         pipeline_mode=pl.Buffered(1)),
            pl.BlockSpec((tm, HEAD_DIM), lambda i: (i % seq_tiles, 0)),
            pl.BlockSpec((tm, HEAD_DIM), lambda i: (i % seq_tiles, 0)),
            row1(HEAD_DIM), row1(HEAD_DIM),
            *[full(t) for t in tables],
            row1(GROUP_WIDTH),
        ],
        out_specs=[
            pl.BlockSpec((tm, moba_width), lambda i: (i, 0)),
            pl.BlockSpec((tm // MOBA_BLOCK, 1, GROUP_WIDTH), lambda i: (i, 0, 0)),
            pl.BlockSpec((tm, GROUP_WIDTH), lambda i: (i, 0)),
            pl.BlockSpec((None, GROUP_WIDTH, tm), lambda i: (i, 0, 0)),
            pl.BlockSpec((None, GROUP_WIDTH, tm), lambda i: (i, 0, 0)),
        ],
        out_shape=[
            jax.ShapeDtypeStruct((m, moba_width), BF16),
            jax.ShapeDtypeStruct((m // MOBA_BLOCK, 1, GROUP_WIDTH), F32),
            jax.ShapeDtypeStruct((m, GROUP_WIDTH), BF16),
            jax.ShapeDtypeStruct((m // tm, GROUP_WIDTH, tm), BF16),
            jax.ShapeDtypeStruct((m // tm, GROUP_WIDTH, tm), BF16),
        ],
        scratch_shapes=[pltpu.VMEM((d_model, width), BF16),
                        pltpu.VMEM((tm, RET_SECTIONS * GROUP_WIDTH), BF16),
                        pltpu.VMEM((RET_HEADS, HEAD_DIM, tm), BF16),
                        pltpu.VMEM((RET_HEADS, HEAD_DIM, HEAD_DIM), F32)],
        compiler_params=pltpu.CompilerParams(
            dimension_semantics=("arbitrary",), vmem_limit_bytes=VMEM_LIMIT),
        name="in_projection",
    )(x2, norm_g, w_in, cos, sin, q_g, k_g, *tables, ret_g)


def _moba_kernel(rb_ref, qt_ref, k_ref, vt_ref, z_ref, km_ref, oh_ref, bias_ref, o_ref,
                 qaug_scr, m_scr, acc_scr, s_scr, tmax_scr):
    blk = MOBA_BLOCK
    tile = MOBA_TILE
    sub = SUBLANES
    ti = pl.program_id(1)
    n_heads = MOBA_HEADS
    sel_rows = -(-(k_ref.shape[0] // blk) // BF16_ROWS) * BF16_ROWS
    blk_id = lax.broadcasted_iota(jnp.int32, (sel_rows, tile), 0)
    blk_f = blk_id.astype(F32)
    q_blk = 2 * ti + lax.broadcasted_iota(jnp.int32, (sel_rows, tile), 1) // blk
    unused = jnp.zeros((SEL_LANES - sel_rows, tile), BF16)
    past = blk_id < q_blk
    far = (q_blk - blk_id) >= N_NEAR_BLOCKS
    own = blk_id == q_blk
    spread = lambda a: jnp.broadcast_to(a[None], (sel_rows // sub,) + a.shape).reshape(sel_rows, tile)

    def key_rows(step):
        return pl.ds(pl.multiple_of((ti - step) * tile, tile), tile)

    def bias_tile(h, step):
        def part(offset):
            idx = jnp.where(offset < 0, N_NEAR_BLOCKS, jnp.minimum(offset, N_NEAR_BLOCKS))
            return bias_ref[h, idx]
        diag = part(2 * step)
        return jnp.concatenate([jnp.concatenate([diag, part(2 * step + 1)], axis=1),
                                jnp.concatenate([part(2 * step - 1), diag], axis=1)], axis=0)

    def key_tile(step, h):
        rows = key_rows(step)
        return jnp.concatenate([k_ref[rows, h * HEAD_DIM:(h + 1) * HEAD_DIM], oh_ref[rows, :]], axis=1)

    def scores(step, slot, h, with_bias):
        s = jnp.dot(key_tile(step, h), qaug_scr[h], preferred_element_type=F32)
        if with_bias:
            s = s + bias_tile(h, step)
        s_scr[slot, h] = s
        tmax_scr[slot, h] = jnp.max(s.reshape(tile // sub, sub, tile), axis=0)

    def own_scores(h):
        k_aug = key_tile(0, h)
        diag = bias_ref[h, 0]
        first = jnp.dot(k_aug[:blk], qaug_scr[h], preferred_element_type=F32)
        first = first + jnp.concatenate([diag, bias_ref[h, 1]], axis=1)
        second = jnp.dot(k_aug[blk:], qaug_scr[h, :, blk:], preferred_element_type=F32) + diag
        s_scr[0, h, :blk, :] = first
        s_scr[0, h, blk:, blk:] = second
        t_first = jnp.max(first.reshape(blk // sub, sub, tile), axis=0)
        t_second = jnp.max(second.reshape(blk // sub, sub, blk), axis=0)
        tmax_scr[0, h] = jnp.concatenate(
            [t_first[:, :blk], jnp.maximum(t_first[:, blk:], t_second)], axis=1)


    for h in range(n_heads):
        cols = slice(h * HEAD_DIM, (h + 1) * HEAD_DIM)
        q_t = qt_ref[cols, :]
        km = km_ref[0, :sel_rows, cols]
        km_hi = km.astype(BF16)
        km_lo = (km - km_hi.astype(F32)).astype(BF16)
        gate2 = jnp.dot(jnp.concatenate([km_hi, km_lo], axis=0), q_t, preferred_element_type=F32)
        gate = gate2[:sel_rows] + gate2[sel_rows:]
        bits = lax.bitcast_convert_type(gate2[:sub, :blk], jnp.uint32)
        zero = lax.bitcast_convert_type((bits >> 16) >> 16, F32)
        fill = jnp.broadcast_to((NEG_INF + zero)[None], (blk // sub, sub, blk)).reshape(blk, blk)
        s_scr[0, h, blk:, :blk] = fill
        g = jnp.where(past, gate, MASKED_GATE)
        for _ in range(MOBA_TOPK):
            top = jnp.max(g, axis=0, keepdims=True)
            first = jnp.min(jnp.where(g == top, blk_f, 1e9), axis=0, keepdims=True)
            g = jnp.where(blk_f == first, MASKED_GATE, g)
        take = (past & (g == MASKED_GATE)) | own
        far_c = jnp.full((sub, tile), rb_ref[(N_BUCKETS - 1) * n_heads + h] * LOG2E, F32)
        far_hi = far_c.astype(BF16).astype(F32)
        mask_hi = jnp.where(take, jnp.where(far, spread(far_hi), 0.0), NEG_INF)
        mask_lo = jnp.where(far, spread(far_c - far_hi), 0.0)
        qaug_scr[h] = jnp.concatenate(
            [q_t, mask_hi.astype(BF16), unused, mask_lo.astype(BF16), unused], axis=0)

    ones_rows = jnp.ones((BF16_ROWS, tile), BF16)

    def accumulate(step, slot, h):
        s = s_scr[slot, h].reshape(tile // sub, sub, tile)
        m_prev = m_scr[h]
        m_tile = jnp.max(tmax_scr[slot, h], axis=0, keepdims=True)
        m_next = jnp.maximum(m_prev, jnp.broadcast_to(m_tile, m_prev.shape))
        alpha = jnp.exp2(m_prev - m_next)
        p = jnp.exp2(s - m_next[None])
        m_scr[h] = m_next
        vt_aug = jnp.concatenate([vt_ref[ti - step, h * HEAD_DIM:(h + 1) * HEAD_DIM, :], ones_rows],
                                 axis=0)
        pv = jnp.dot(vt_aug, p.reshape(tile, tile).astype(BF16),
                     preferred_element_type=F32)
        acc = acc_scr[h].reshape(ACC_ROWS // sub, sub, tile) * alpha[None]
        acc_scr[h] = acc.reshape(ACC_ROWS, tile) + pv

    def loop_body(next_with_bias):
        def one_step(step, slot):
            for h in range(n_heads):
                scores(step + 1, 1 - slot, h, next_with_bias)
                accumulate(step, slot, h)

        def body(step, carry):
            for slot in (0, 1):
                pl.when((step & 1) == slot)(functools.partial(one_step, step, slot))
            return carry
        return body

    n_biased = jnp.minimum(ti, MOBA_NEAR_STEPS - 1)
    for h in range(n_heads):
        own_scores(h)
    m_scr[...] = jnp.full(m_scr.shape, NEG_INF, F32)

    @pl.when(ti == 0)
    def _():
        acc_scr[...] = jnp.zeros(acc_scr.shape, F32)
    lax.fori_loop(0, n_biased, loop_body(True), 0)
    lax.fori_loop(n_biased, ti, loop_body(False), 0)
    for h in range(n_heads):
        accumulate(ti, ti & 1, h)
        cols = slice(h * HEAD_DIM, (h + 1) * HEAD_DIM)
        z = z_ref[:, cols].astype(F32)
        inv = 1.0 / acc_scr[h, HEAD_DIM:HEAD_DIM + 1, :]
        out = (acc_scr[h, :HEAD_DIM, :] * inv).T
        half_z = 0.5 * z
        o_ref[:, cols] = ((half_z + half_z * jnp.tanh(half_z)) * out).astype(BF16)


def _moba(proj, q_t, v_t, kmean, bias_tiles, rel_bias, batch, seq):
    m = proj.shape[0]
    tile = MOBA_TILE
    nq = seq // tile
    row_blk = np.arange(seq)[:, None] // MOBA_BLOCK
    onehot = jnp.asarray((np.arange(HEAD_DIM)[None, :] & (SEL_LANES - 1)) == row_blk, BF16)
    rows = lambda sec: pl.BlockSpec((tile, GROUP_WIDTH), lambda b, i: (b * nq + i, sec))
    whole = lambda sec: pl.BlockSpec((seq, GROUP_WIDTH), lambda b, i: (b, sec))
    return pl.pallas_call(
        _moba_kernel,
        grid=(batch, nq),
        in_specs=[pl.BlockSpec(memory_space=pltpu.SMEM),
                  pl.BlockSpec((None, GROUP_WIDTH, tile), lambda b, i: (b * nq + i, 0, 0)),
                  whole(0),
                  pl.BlockSpec((nq, GROUP_WIDTH, tile), lambda b, i: (b, 0, 0),
                               pipeline_mode=pl.Buffered(1)),
                  rows(1),
                  pl.BlockSpec((1, SEL_LANES, GROUP_WIDTH), lambda b, i: (b, 0, 0)),
                  pl.BlockSpec(onehot.shape, lambda b, i: (0, 0), pipeline_mode=pl.Buffered(1)),
                  pl.BlockSpec(bias_tiles.shape, lambda b, i: (0, 0, 0, 0),
                               pipeline_mode=pl.Buffered(1))],
        out_specs=pl.BlockSpec((tile, GROUP_WIDTH), lambda b, i: (b * nq + i, 0)),
        out_shape=jax.ShapeDtypeStruct((m, GROUP_WIDTH), BF16),
        scratch_shapes=[pltpu.VMEM((MOBA_HEADS, 2 * HEAD_DIM, tile), BF16),
                        pltpu.VMEM((MOBA_HEADS, SUBLANES, tile), F32),
                        pltpu.VMEM((MOBA_HEADS, ACC_ROWS, tile), F32),
                        pltpu.VMEM((2, MOBA_HEADS, tile, tile), F32),
                        pltpu.VMEM((2, MOBA_HEADS, SUBLANES, tile), F32)],
        compiler_params=pltpu.CompilerParams(
            dimension_semantics=("arbitrary", "arbitrary"), vmem_limit_bytes=VMEM_LIMIT),
        name="moba",
    )(rel_bias.astype(F32).reshape(-1), q_t, proj, v_t, proj, kmean, onehot, bias_tiles)


def _outproj_kernel(x_ref, r_ref, a_ref, wo_ref, p_ref, wp_ref, pg_ref, wg_ref, o_ref,
                    wo_scr, wp_scr, wg_scr):
    _cast_weights_once([(wo_ref, wo_scr), (wp_ref, wp_scr), (wg_ref, wg_scr)])

    def unit_rms(t):
        return t * lax.rsqrt(jnp.mean(t * t, axis=-1, keepdims=True) + EPS)

    slab = x_ref.shape[0] // OUT_SLABS
    slabs = [slice(c * slab, (c + 1) * slab) for c in range(OUT_SLABS)]
    for rows in slabs:
        y = (jnp.dot(r_ref[rows, :], wo_scr[:GROUP_WIDTH, :], preferred_element_type=F32)
             + jnp.dot(a_ref[rows, :], wo_scr[GROUP_WIDTH:, :], preferred_element_type=F32))
        o_ref[rows, :] = x_ref[rows, :] + y
    for rows in slabs:
        e = unit_rms(jnp.dot(p_ref[rows, :].astype(BF16), wp_scr[...], preferred_element_type=F32))
        x1 = o_ref[rows, :]
        gate = jax.nn.sigmoid(jnp.dot(unit_rms(x1).astype(BF16), wg_scr[...],
                                      preferred_element_type=F32))
        o_ref[rows, :] = x1 + gate * (e * pg_ref[...])


def _outproj(x2, r_out, m_out, layer, w_out, p3, w_ple, ple_g, w_gate):
    m, d_model = x2.shape
    tm = OUT_ROW_TILE
    rows = lambda a: pl.BlockSpec((tm, a.shape[1]), lambda i: (i, 0))
    full = lambda a: pl.BlockSpec((None,) + a.shape[1:], lambda i: (layer, 0, 0),
                                  pipeline_mode=pl.Buffered(1))
    return pl.pallas_call(
        _outproj_kernel,
        grid=(m // tm,),
        in_specs=[rows(x2), rows(r_out), rows(m_out), full(w_out),
                  pl.BlockSpec((None, tm, p3.shape[2]), lambda i: (layer, i, 0)), full(w_ple),
                  full(ple_g), full(w_gate)],
        out_specs=pl.BlockSpec((tm, d_model), lambda i: (i, 0)),
        out_shape=jax.ShapeDtypeStruct((m, d_model), F32),
        scratch_shapes=[pltpu.VMEM(w.shape[1:], BF16) for w in (w_out, w_ple, w_gate)],
        compiler_params=pltpu.CompilerParams(
            dimension_semantics=("arbitrary",), vmem_limit_bytes=VMEM_LIMIT),
        name="out_projection",
    )(x2, r_out, m_out, w_out, p3, w_ple, ple_g, w_gate)


def _rotary_tables(seq):
    half = HEAD_DIM // 2
    inv = np.float32(ROPE_THETA) ** (-np.arange(half, dtype=np.float32) / np.float32(half))
    ang = np.arange(seq, dtype=np.float32)[:, None] * inv[None, :]
    cos, sin = np.cos(ang), np.sin(ang)
    return (jnp.asarray(np.concatenate([cos, cos], axis=1), F32),
            jnp.asarray(np.concatenate([-sin, sin], axis=1), F32))


def kernel(x, p, norm_g, w_in, ret_norm_g, q_norm_g, k_norm_g, rel_bias, w_out, w_ple, ple_norm_g, w_ple_gate):
    batch, seq, d_model = x.shape
    depth = w_in.shape[0]
    m = batch * seq
    assert w_in.shape[2] == N_SECTIONS * GROUP_WIDTH and w_out.shape[1] == 2 * GROUP_WIDTH
    assert rel_bias.shape == (N_BUCKETS, MOBA_HEADS)
    assert seq % ROW_TILE == 0 and ROW_TILE % MOBA_BLOCK == 0 and seq % MOBA_TILE == 0
    assert m % OUT_ROW_TILE == 0
    assert ROW_TILE == MOBA_TILE
    assert ROW_TILE % RET_CHUNK == 0 and seq // MOBA_BLOCK <= SEL_LANES

    cos, sin = _rotary_tables(seq)
    bias_tiles = _bias_tiles(rel_bias)
    x2 = x.reshape(m, d_model)
    nb = seq // MOBA_BLOCK
    stack = lambda g: g.reshape(depth, 1, -1)
    p3 = p.reshape(depth, m, -1)
    for i in range(depth):
        proj, kmean, r_out, q_t, v_t = _inproj(x2, i, stack(norm_g), w_in, cos, sin,
                                     stack(q_norm_g), stack(k_norm_g), stack(ret_norm_g), seq)
        kmean = jnp.pad(kmean.reshape(batch, nb, GROUP_WIDTH), ((0, 0), (0, SEL_LANES - nb), (0, 0)))
        m_out = _moba(proj, q_t, v_t, kmean, bias_tiles, rel_bias, batch, seq)
        x2 = _outproj(x2, r_out, m_out, i, w_out, p3, w_ple, stack(ple_norm_g), w_ple_gate)
    return x2.reshape(batch, seq, d_model)
```
